```python
import math
import jax
import jax.numpy as jnp
from jax import lax
import numpy as np

D_MODEL = 1024
BATCH = 8
SEQ = 8192
DEPTH = 2

CTX_LEN = 256
GRID_W = 64
N_HEADS = 8
N_KV_HEADS = 2
HEAD_DIM = 64
Q_PER_KV = N_HEADS // N_KV_HEADS
W_ATTN = N_HEADS * HEAD_DIM
W_KV = N_KV_HEADS * HEAD_DIM
WINDOW = 128
BLOCK = 128
ROPE_BASE = 10000.0
W_S5 = 256
S5_GROUP = 16
S5_GROUPS = W_S5 // S5_GROUP
S5_STATE = 64
DT_MIN = 0.001
DT_MAX = 0.1
ML_HEADS = 4
ML_HEAD_DIM = 64
W_ML = ML_HEADS * ML_HEAD_DIM
ML_CHUNK = 64
N_BRANCH = 3
D_FF = 2816
CONV_W = 3
NORM_EPS = 1e-6
IN_SPLITS = (W_ATTN, W_KV, W_KV, W_S5, W_ML, W_ML, W_ML, W_ML, 4 * ML_HEADS, N_BRANCH * D_MODEL)
D_IN = W_ATTN + 2 * W_KV + W_S5 + 4 * W_ML + 4 * ML_HEADS + N_BRANCH * D_MODEL

kernel_name = 'hybrid_prefix_swa_s5_mlstm_block'


def _rmsnorm(x, g):
    x32 = x.astype(jnp.float32)
    y = x32 * lax.rsqrt(jnp.mean(x32 * x32, axis=-1, keepdims=True) + NORM_EPS)
    return (y * g.astype(jnp.float32)).astype(x.dtype)


def _modulate(h, shift, scale):
    return h * (1 + scale) + shift


def _split_in(z):
    offsets, run = [], 0
    for size in IN_SPLITS[:-1]:
        run += size
        offsets.append(run)
    return jnp.split(z, offsets, axis=-1)


def _axial_angles(n_tok):
    rows = n_tok // GRID_W
    row = jnp.broadcast_to(jnp.arange(rows, dtype=jnp.float32)[:, None], (rows, GRID_W)).reshape(-1)
    col = jnp.broadcast_to(jnp.arange(GRID_W, dtype=jnp.float32)[None, :], (rows, GRID_W)).reshape(-1)
    n_freq = HEAD_DIM // 4
    inv_freq = ROPE_BASE ** (-jnp.arange(n_freq, dtype=jnp.float32) / n_freq)
    return row[:, None] * inv_freq[None, :], col[:, None] * inv_freq[None, :]


def _rotate(u, ang):
    half = u.shape[-1] // 2
    cos = jnp.cos(ang)[None, :, None, :].astype(u.dtype)
    sin = jnp.sin(ang)[None, :, None, :].astype(u.dtype)
    u1, u2 = u[..., :half], u[..., half:]
    return jnp.concatenate([u1 * cos - u2 * sin, u2 * cos + u1 * sin], axis=-1)


def _axial_rope(t, ang_row, ang_col):
    half = HEAD_DIM // 2
    return jnp.concatenate([_rotate(t[..., :half], ang_row), _rotate(t[..., half:], ang_col)], axis=-1)


def _attn_latent(q, k, v, kc, vc, sink):
    b, n_tok = q.shape[0], q.shape[1]
    nb = n_tok // BLOCK
    scale = HEAD_DIM ** -0.5
    pad = ((0, 0), (BLOCK, BLOCK), (0, 0), (0, 0))
    kp, vp = jnp.pad(k, pad), jnp.pad(v, pad)
    qb = jnp.moveaxis(q.reshape(b, nb, BLOCK, N_KV_HEADS, Q_PER_KV, HEAD_DIM), 1, 0)
    sink_col = jnp.broadcast_to(sink[None, :, :, None, None], (b, N_KV_HEADS, Q_PER_KV, BLOCK, 1))
    offs_q = jnp.arange(BLOCK)
    offs_k = jnp.arange(3 * BLOCK) - BLOCK

    def one_block(args):
        qj, j = args
        start = j * BLOCK
        kj = lax.dynamic_slice_in_dim(kp, start, 3 * BLOCK, axis=1)
        vj = lax.dynamic_slice_in_dim(vp, start, 3 * BLOCK, axis=1)
        qpos = start + offs_q
        kpos = start + offs_k
        valid = (jnp.abs(qpos[:, None] - kpos[None, :]) <= WINDOW) & (kpos >= 0)[None, :] & (kpos < n_tok)[None, :]
        s_loc = jnp.einsum('bqhgd,bkhd->bhgqk', qj, kj).astype(jnp.float32) * scale
        s_loc = jnp.where(valid, s_loc, -jnp.inf)
        s_ctx = jnp.einsum('bqhgd,bchd->bhgqc', qj, kc).astype(jnp.float32) * scale
        p = jax.nn.softmax(jnp.concatenate([s_loc, s_ctx, sink_col], axis=-1), axis=-1).astype(v.dtype)
        return (jnp.einsum('bhgqk,bkhd->bqhgd', p[..., :3 * BLOCK], vj)
                + jnp.einsum('bhgqc,bchd->bqhgd', p[..., 3 * BLOCK:-1], vc))

    out = lax.map(one_block, (qb, jnp.arange(nb)))
    return jnp.moveaxis(out, 0, 1).reshape(b, n_tok, W_ATTN)


def _attn_context(qc, kc, vc, sink):
    b, n_ctx = qc.shape[0], qc.shape[1]
    s = jnp.einsum('bqhgd,bchd->bhgqc', qc, kc).astype(jnp.float32) * HEAD_DIM ** -0.5
    sink_col = jnp.broadcast_to(sink[None, :, :, None, None], (b, N_KV_HEADS, Q_PER_KV, n_ctx, 1))
    p = jax.nn.softmax(jnp.concatenate([s, sink_col], axis=-1), axis=-1).astype(vc.dtype)
    return jnp.einsum('bhgqc,bchd->bqhgd', p[..., :-1], vc).reshape(b, n_ctx, W_ATTN)


def _cmul(ar, ai, br, bi):
    return ar * br - ai * bi, ar * bi + ai * br


def _s5_scan(bu_re, bu_im, lam_re, lam_im, dre, dim, init=None):
    n_t = bu_re.shape[1]
    a_re = jnp.broadcast_to(lam_re, (1, n_t) + lam_re.shape)
    a_im = jnp.broadcast_to(lam_im, (1, n_t) + lam_im.shape)

    def combine(e1, e2):
        a1r, a1i, b1r, b1i = e1
        a2r, a2i, b2r, b2i = e2
        ar, ai = _cmul(a2r, a2i, a1r, a1i)
        br, bi = _cmul(a2r, a2i, b1r, b1i)
        return ar, ai, br + b2r, bi + b2i

    _, _, s_re, s_im = lax.associative_scan(combine, (a_re, a_im, bu_re, bu_im), axis=1)
    if init is not None:
        steps = jnp.arange(1, n_t + 1, dtype=jnp.float32)[:, None, None]
        pmag = jnp.exp(steps * dre)
        p_re, p_im = pmag * jnp.cos(steps * dim), pmag * jnp.sin(steps * dim)
        i_re, i_im = _cmul(p_re[None], p_im[None], init[0][:, None], init[1][:, None])
        s_re, s_im = s_re + i_re, s_im + i_im
    return s_re, s_im


def _s5_direction(ux, uc, a_re, a_im, log_dt, b_re, b_im, reverse):
    dt = jnp.exp(log_dt)[:, None]
    dre, dim = dt * a_re, dt * a_im
    mag = jnp.exp(dre)
    lam_re, lam_im = mag * jnp.cos(dim), mag * jnp.sin(dim)
    den = a_re * a_re + a_im * a_im
    coef_re = ((lam_re - 1.0) * a_re + lam_im * a_im) / den
    coef_im = (lam_im * a_re - (lam_re - 1.0) * a_im) / den
    bb_re, bb_im = _cmul(coef_re[..., None], coef_im[..., None], b_re, b_im)

    def drive(u):
        br = jnp.einsum('btgc,gpc->btgp', u, bb_re)
        bi = jnp.einsum('btgc,gpc->btgp', u, bb_im)
        return (jnp.flip(br, 1), jnp.flip(bi, 1)) if reverse else (br, bi)

    sc_re, sc_im = _s5_scan(*drive(uc), lam_re, lam_im, dre, dim)
    sx_re, sx_im = _s5_scan(*drive(ux), lam_re, lam_im, dre, dim, init=(sc_re[:, -1], sc_im[:, -1]))
    if reverse:
        sx_re, sx_im, sc_re, sc_im = (jnp.flip(a, 1) for a in (sx_re, sx_im, sc_re, sc_im))
    return sx_re, sx_im, sc_re, sc_im


def _s5_branch(ux, uc, a_re, a_im, log_dt, b_re, b_im, c_re, c_im, d_skip, glu_w, glu_b, with_ctx):
    f32 = jnp.float32

    def grouped(u):
        b, t, _ = u.shape
        return u.astype(f32).reshape(b, t, S5_GROUPS, S5_GROUP)

    gx, gc = grouped(ux), grouped(uc)
    a_re, a_im, log_dt, b_re, b_im, c_re, c_im = (p.astype(f32) for p in (a_re, a_im, log_dt, b_re, b_im, c_re, c_im))
    fwd = _s5_direction(gx, gc, a_re[0], a_im[0], log_dt[0], b_re, b_im, False)
    bwd = _s5_direction(gx, gc, a_re[1], a_im[1], log_dt[1], b_re, b_im, True)
    dsk = d_skip.astype(f32).reshape(S5_GROUPS, S5_GROUP)
    w_glu, b_glu = glu_w.astype(f32), glu_b.astype(f32)

    def readout(s_re, s_im, u, like):
        b, t = u.shape[0], u.shape[1]
        y = (jnp.einsum('btgp,gcp->btgc', s_re, c_re) - jnp.einsum('btgp,gcp->btgc', s_im, c_im)
             + dsk * u)
        y = jax.nn.gelu(y.reshape(b, t, W_S5))
        return (y * jax.nn.sigmoid(y @ w_glu + b_glu)).astype(like.dtype)

    y_lat = readout(fwd[0] + bwd[0], fwd[1] + bwd[1], gx, ux)
    y_ctx = readout(fwd[2] + bwd[2], fwd[3] + bwd[3], gc, uc) if with_ctx else None
    return y_lat, y_ctx


def _mlstm_scan(q, k, v, ig, lf, state):
    b, h, t, _ = q.shape
    nc = t // ML_CHUNK

    def chunks(a):
        return jnp.moveaxis(a.reshape((b, h, nc, ML_CHUNK) + a.shape[3:]), 2, 0)

    tril = jnp.tril(jnp.ones((ML_CHUNK, ML_CHUNK), dtype=bool))

    def step(carry, inp):
        c_mat, n_vec, m = carry
        qc, kc, vc, ic, fc = inp
        bcum = jnp.cumsum(fc, axis=-1)
        a_inter = bcum + m[..., None]
        d_intra = jnp.where(tril, bcum[..., :, None] - bcum[..., None, :] + ic[..., None, :], -jnp.inf)
        m_t = jnp.maximum(a_inter, jnp.max(d_intra, axis=-1))
        w_inter = jnp.exp(a_inter - m_t)
        w_intra = jnp.exp(d_intra - m_t[..., None]) * jnp.einsum('bhtd,bhsd->bhts', qc, kc)
        num = (w_inter[..., None] * jnp.einsum('bhed,bhtd->bhte', c_mat, qc)
               + jnp.einsum('bhts,bhse->bhte', w_intra, vc))
        den = w_inter * jnp.einsum('bhd,bhtd->bht', n_vec, qc) + jnp.sum(w_intra, axis=-1)
        h_out = num / jnp.maximum(jnp.abs(den), jnp.exp(-m_t))[..., None]
        b_end = bcum[..., -1]
        g = b_end[..., None] - bcum + ic
        m_new = jnp.maximum(b_end + m, jnp.max(g, axis=-1))
        decay = jnp.exp(b_end + m - m_new)
        w_s = jnp.exp(g - m_new[..., None])
        c_new = decay[..., None, None] * c_mat + jnp.einsum('bhs,bhse,bhsd->bhed', w_s, vc, kc)
        n_new = decay[..., None] * n_vec + jnp.einsum('bhs,bhsd->bhd', w_s, kc)
        return (c_new, n_new, m_new), h_out

    final, hs = lax.scan(step, state, (chunks(q), chunks(k), chunks(v), chunks(ig), chunks(lf)))
    return jnp.moveaxis(hs, 0, 2).reshape(b, h, t, v.shape[-1]), final


def _mlstm_branch(qx, kx, vx, ox, gx, qc, kc, vc, oc, gc, ig_b, fg_b, norm_g, with_ctx):
    f32 = jnp.float32
    k_scale = ML_HEAD_DIM ** -0.5

    def heads(a):
        b, t, _ = a.shape
        return a.astype(f32).reshape(b, t, ML_HEADS, ML_HEAD_DIM).transpose(0, 2, 1, 3)

    def gate_preacts(g):
        b, t, _ = g.shape
        return g.astype(f32).reshape(b, t, 4, ML_HEADS).transpose(2, 0, 3, 1)

    lat = (heads(qx), heads(kx) * k_scale, heads(vx))
    ctxs = (heads(qc), heads(kc) * k_scale, heads(vc))
    g_lat, g_ctx = gate_preacts(gx), gate_preacts(gc)
    ig_b, fg_b = ig_b.astype(f32), fg_b.astype(f32)
    b = qx.shape[0]
    zero = (jnp.zeros((b, ML_HEADS, ML_HEAD_DIM, ML_HEAD_DIM), f32),
            jnp.zeros((b, ML_HEADS, ML_HEAD_DIM), f32),
            jnp.zeros((b, ML_HEADS), f32))
    h_lat, h_ctx = [], []
    for d in range(2):
        def prep(qkv, g):
            ig = g[2 * d] + ig_b[d][:, None]
            lf = jax.nn.log_sigmoid(g[2 * d + 1] + fg_b[d][:, None])
            arrs = (qkv[0], qkv[1], qkv[2], ig, lf)
            return tuple(jnp.flip(a, 2) for a in arrs) if d == 1 else arrs
        hc_d, state = _mlstm_scan(*prep(ctxs, g_ctx), zero)
        hx_d, _ = _mlstm_scan(*prep(lat, g_lat), state)
        if d == 1:
            hc_d, hx_d = jnp.flip(hc_d, 2), jnp.flip(hx_d, 2)
        h_lat.append(hx_d)
        h_ctx.append(hc_d)

    def readout(h, o):
        h = h * lax.rsqrt(jnp.mean(h * h, axis=-1, keepdims=True) + NORM_EPS)
        bb, _, t, _ = h.shape
        h = h.transpose(0, 2, 1, 3).reshape(bb, t, W_ML) * norm_g.astype(f32)
        return (h * jax.nn.sigmoid(o.astype(f32))).astype(o.dtype)

    y_lat = readout(h_lat[0] + h_lat[1], ox)
    y_ctx = readout(h_ctx[0] + h_ctx[1], oc) if with_ctx else None
    return y_lat, y_ctx


def _token_mixers(hx, hc, ang_row, ang_col, w_in, attn_sink, s5_a_re, s5_a_im, s5_log_dt, s5_b_re, s5_b_im,
                  s5_c_re, s5_c_im, s5_d, s5_glu_w, s5_glu_b, ml_igate_b, ml_fgate_b, ml_norm_g,
                  w_branch_attn, w_branch_s5, w_branch_ml, w_out, with_ctx):
    b, n_tok, _ = hx.shape
    n_ctx = hc.shape[1]
    qa_x, ka_x, va_x, us_x, qm_x, km_x, vm_x, om_x, gm_x, gb_x = _split_in(hx @ w_in)
    qa_c, ka_c, va_c, us_c, qm_c, km_c, vm_c, om_c, gm_c, gb_c = _split_in(hc @ w_in)
    q_lat = _axial_rope(qa_x.reshape(b, n_tok, N_HEADS, HEAD_DIM), ang_row, ang_col)
    q_lat = q_lat.reshape(b, n_tok, N_KV_HEADS, Q_PER_KV, HEAD_DIM)
    k_lat = _axial_rope(ka_x.reshape(b, n_tok, N_KV_HEADS, HEAD_DIM), ang_row, ang_col)
    v_lat = va_x.reshape(b, n_tok, N_KV_HEADS, HEAD_DIM)
    k_ctx = ka_c.reshape(b, n_ctx, N_KV_HEADS, HEAD_DIM)
    v_ctx = va_c.reshape(b, n_ctx, N_KV_HEADS, HEAD_DIM)
    sink = attn_sink.astype(jnp.float32).reshape(N_KV_HEADS, Q_PER_KV)
    ya_x = _attn_latent(q_lat, k_lat, v_lat, k_ctx, v_ctx, sink)
    ys_x, ys_c = _s5_branch(us_x, us_c, s5_a_re, s5_a_im, s5_log_dt, s5_b_re, s5_b_im, s5_c_re, s5_c_im,
                            s5_d, s5_glu_w, s5_glu_b, with_ctx)
    ym_x, ym_c = _mlstm_branch(qm_x, km_x, vm_x, om_x, gm_x, qm_c, km_c, vm_c, om_c, gm_c,
                               ml_igate_b, ml_fgate_b, ml_norm_g, with_ctx)

    def merge(ya, ys, ym, gates):
        g_a, g_s, g_m = jnp.split(gates, N_BRANCH, axis=-1)
        y = (jax.nn.sigmoid(g_a) * (ya @ w_branch_attn) + jax.nn.sigmoid(g_s) * (ys @ w_branch_s5)
             + jax.nn.sigmoid(g_m) * (ym @ w_branch_ml))
        return y @ w_out

    out_x = merge(ya_x, ys_x, ym_x, gb_x)
    if not with_ctx:
        return out_x, None
    q_ctx = qa_c.reshape(b, n_ctx, N_KV_HEADS, Q_PER_KV, HEAD_DIM)
    ya_c = _attn_context(q_ctx, k_ctx, v_ctx, sink)
    return out_x, merge(ya_c, ys_c, ym_c, gb_c)


def _conv_ffn(h, w_up, conv_w, conv_b, w_down):
    u = h @ w_up
    up = jnp.pad(u, ((0, 0), (1, 1), (0, 0)))
    u = up[:, :-2] * conv_w[0] + up[:, 1:-1] * conv_w[1] + up[:, 2:] * conv_w[2] + conv_b
    a, v = jnp.split(u, 2, axis=-1)
    return (jax.nn.silu(a) * v) @ w_down


def setup_inputs(seed: int = 0) -> dict:
    key = jax.random.key(seed)
    ks = iter(jax.random.split(key, 40))
    f32 = jnp.float32

    def nrm(shape, scale):
        return scale * jax.random.normal(next(ks), shape, f32)

    L, D = DEPTH, D_MODEL
    inp = {}
    inp['x'] = nrm((BATCH, SEQ, D), 1.0)
    inp['c'] = nrm((BATCH, D), 1.0)
    inp['ctx'] = nrm((BATCH, CTX_LEN, D), 1.0)
    inp['c_ctx'] = nrm((D,), 1.0)
    inp['mod_w'] = nrm((L, D, 6 * D), D ** -0.5)
    inp['mod_b'] = nrm((L, 6 * D), 0.02)
    inp['norm1_g'] = 1.0 + nrm((L, D), 0.02)
    inp['norm2_g'] = 1.0 + nrm((L, D), 0.02)
    inp['w_in'] = nrm((L, D, D_IN), D ** -0.5)
    inp['attn_sink'] = nrm((L, N_HEADS), 0.5)
    inp['s5_a_re'] = -0.5 + nrm((L, 2, S5_GROUPS, S5_STATE), 0.01)
    inp['s5_a_im'] = math.pi * jnp.arange(S5_STATE, dtype=f32) + nrm((L, 2, S5_GROUPS, S5_STATE), 0.01)
    inp['s5_log_dt'] = jax.random.uniform(next(ks), (L, 2, S5_GROUPS), f32, math.log(DT_MIN), math.log(DT_MAX))
    inp['s5_b_re'] = nrm((L, S5_GROUPS, S5_STATE, S5_GROUP), (2 * S5_GROUP) ** -0.5)
    inp['s5_b_im'] = nrm((L, S5_GROUPS, S5_STATE, S5_GROUP), (2 * S5_GROUP) ** -0.5)
    inp['s5_c_re'] = nrm((L, S5_GROUPS, S5_GROUP, S5_STATE), S5_STATE ** -0.5)
    inp['s5_c_im'] = nrm((L, S5_GROUPS, S5_GROUP, S5_STATE), S5_STATE ** -0.5)
    inp['s5_d'] = nrm((L, W_S5), 1.0)
    inp['s5_glu_w'] = nrm((L, W_S5, W_S5), W_S5 ** -0.5)
    inp['s5_glu_b'] = nrm((L, W_S5), 0.02)
    inp['ml_igate_b'] = nrm((L, 2, ML_HEADS), 0.1)
    inp['ml_fgate_b'] = jnp.linspace(3.0, 6.0, ML_HEADS, dtype=f32) + nrm((L, 2, ML_HEADS), 0.1)
    inp['ml_norm_g'] = 1.0 + nrm((L, W_ML), 0.02)
    inp['w_branch_attn'] = nrm((L, W_ATTN, D), W_ATTN ** -0.5)
    inp['w_branch_s5'] = nrm((L, W_S5, D), W_S5 ** -0.5)
    inp['w_branch_ml'] = nrm((L, W_ML, D), W_ML ** -0.5)
    inp['w_out'] = nrm((L, D, D), D ** -0.5)
    inp['ffn_w_up'] = nrm((L, D, 2 * D_FF), D ** -0.5)
    inp['ffn_conv_w'] = nrm((L, CONV_W, 2 * D_FF), CONV_W ** -0.5)
    inp['ffn_conv_b'] = nrm((L, 2 * D_FF), 0.02)
    inp['ffn_w_down'] = nrm((L, D_FF, D), D_FF ** -0.5)
    inp['final_norm_g'] = 1.0 + nrm((D,), 0.02)
    return inp


def reference(x, c, ctx, c_ctx, mod_w, mod_b, norm1_g, norm2_g, w_in, attn_sink, s5_a_re, s5_a_im, s5_log_dt,
              s5_b_re, s5_b_im, s5_c_re, s5_c_im, s5_d, s5_glu_w, s5_glu_b, ml_igate_b, ml_fgate_b, ml_norm_g,
              w_branch_attn, w_branch_s5, w_branch_ml, w_out, ffn_w_up, ffn_conv_w, ffn_conv_b, ffn_w_down,
              final_norm_g):
    ang_row, ang_col = _axial_angles(x.shape[1])
    xc = ctx
    for layer in range(DEPTH):
        with_ctx = layer < DEPTH - 1
        mod_x = (jax.nn.silu(c) @ mod_w[layer] + mod_b[layer])[:, None, :]
        mod_c = jax.nn.silu(c_ctx) @ mod_w[layer] + mod_b[layer]
        sh1x, sc1x, g1x, sh2x, sc2x, g2x = jnp.split(mod_x, 6, axis=-1)
        sh1c, sc1c, g1c, sh2c, sc2c, g2c = jnp.split(mod_c, 6, axis=-1)
        hx = _modulate(_rmsnorm(x, norm1_g[layer]), sh1x, sc1x)
        hc = _modulate(_rmsnorm(xc, norm1_g[layer]), sh1c, sc1c)
        mx, mc = _token_mixers(hx, hc, ang_row, ang_col, w_in[layer], attn_sink[layer], s5_a_re[layer],
                               s5_a_im[layer], s5_log_dt[layer], s5_b_re[layer], s5_b_im[layer], s5_c_re[layer],
                               s5_c_im[layer], s5_d[layer], s5_glu_w[layer], s5_glu_b[layer], ml_igate_b[layer],
                               ml_fgate_b[layer], ml_norm_g[layer], w_branch_attn[layer], w_branch_s5[layer],
                               w_branch_ml[layer], w_out[layer], with_ctx)
        x = x + g1x * mx
        x = x + g2x * _conv_ffn(_modulate(_rmsnorm(x, norm2_g[layer]), sh2x, sc2x), ffn_w_up[layer],
                                ffn_conv_w[layer], ffn_conv_b[layer], ffn_w_down[layer])
        if with_ctx:
            xc = xc + g1c * mc
            xc = xc + g2c * _conv_ffn(_modulate(_rmsnorm(xc, norm2_g[layer]), sh2c, sc2c), ffn_w_up[layer],
                                      ffn_conv_w[layer], ffn_conv_b[layer], ffn_w_down[layer])
    return _rmsnorm(x, final_norm_g)
```

```python
import functools
import math

import jax
import jax.numpy as jnp
from jax import lax
from jax.experimental import pallas as pl
from jax.experimental.pallas import tpu as pltpu

F32 = jnp.float32
BF16 = jnp.bfloat16

N_HEADS = 8
N_KV_HEADS = 2
HEAD_DIM = 64
Q_PER_KV = N_HEADS // N_KV_HEADS
W_ATTN = N_HEADS * HEAD_DIM
WINDOW = 128
GRID_W = 64
ROPE_BASE = 10000.0
W_S5 = 256
S5_GROUP = 16
S5_GROUPS = W_S5 // S5_GROUP
S5_STATE = 64
S5_MODES = S5_GROUPS * S5_STATE
ML_HEADS = 4
ML_HEAD_DIM = 64
W_ML = ML_HEADS * ML_HEAD_DIM
N_GATES = 4 * ML_HEADS
N_BRANCH = 3
NORM_EPS = 1e-6

LANES = 128
BF16_SUBLANES = 16
V7X_VMEM_LIMIT_BYTES = 56 * 1024 * 1024

ROW_TILE = 256
ML_CHUNK = 128
S5_TILE = 128
FFN_CHUNK = 256
NEG_BIG = -1e30

Z_QA = 0
Z_KV = Z_QA + W_ATTN
Z_US = Z_KV + 4 * LANES
Z_QKVM = Z_US + W_S5
Z_OM = Z_QKVM + 3 * W_ML
Z_GB = Z_OM + W_ML


def _sigmoid(x):
    return 1.0 / (1.0 + jnp.exp(-x))


def _log_sigmoid(x):
    return jnp.minimum(x, 0.0) - jnp.log(1.0 + jnp.exp(-jnp.abs(x)))


def _gelu_tanh(x):
    return 0.5 * x * (1.0 + jnp.tanh(math.sqrt(2.0 / math.pi) * (x + 0.044715 * (x * x * x))))


def _rms(x):
    return x * lax.rsqrt(jnp.mean(x * x, axis=-1, keepdims=True) + NORM_EPS)


def _dot(a, b):
    return jnp.dot(a, b, preferred_element_type=F32)


def _dot_nt(a, b):
    return lax.dot_general(a, b, (((1,), (1,)), ((), ())), preferred_element_type=F32)


def _dot_tn(a, b):
    return lax.dot_general(a, b, (((0,), (0,)), ((), ())), preferred_element_type=F32)


def _dot_exact(a, b):
    return jnp.dot(a, b, preferred_element_type=F32, precision=lax.Precision.HIGHEST)


def _params(sem, vmem=None):
    return pltpu.CompilerParams(dimension_semantics=sem, vmem_limit_bytes=vmem)


def _mod_kernel(c_ref, w_ref, b_ref, o_ref):
    c = c_ref[...]
    s = (c * _sigmoid(c)).astype(BF16)
    o_ref[0] = _dot(s, w_ref[0]) + b_ref[0]


def _mod_call(c_all, mod_w, mod_b):
    n_layer, d, d6 = mod_w.shape
    tn = d6 // 4
    rows = c_all.shape[0]
    return pl.pallas_call(
        _mod_kernel,
        grid=(n_layer, d6 // tn),
        in_specs=[
            pl.BlockSpec((rows, d), lambda l, j: (0, 0)),
            pl.BlockSpec((1, d, tn), lambda l, j: (l, 0, j)),
            pl.BlockSpec((1, 1, tn), lambda l, j: (l, 0, j)),
        ],
        out_specs=pl.BlockSpec((1, rows, tn), lambda l, j: (l, 0, j)),
        out_shape=jax.ShapeDtypeStruct((n_layer, rows, d6), F32),
        compiler_params=_params(("parallel", "parallel")),
        name="modulation",
    )(c_all, mod_w, mod_b.reshape(n_layer, 1, d6))


def _rope(xb, cos, s1, s2):
    return xb * cos + pltpu.roll(xb, LANES - 16, 1) * s1 + pltpu.roll(xb, 16, 1) * s2


def _in_kernel(x_ref, mod_ref, g_ref, w_ref, wgt_ref, cos_ref, s1_ref, s2_ref,
               qa_ref, kv_ref, us_ref, qkvm_ref, om_ref, gb_ref, gm_ref, gmt_ref, *, d_model):
    d = d_model
    mod = mod_ref[0]
    h = (_rms(x_ref[0]) * g_ref[...] * (1.0 + mod[:, d:2 * d]) + mod[:, 0:d]).astype(BF16)
    cos, s1, s2 = cos_ref[...], s1_ref[...], s2_ref[...]

    def proj(lo, width):
        return _dot(h, w_ref[:, lo:lo + width])

    zq = proj(Z_QA, W_ATTN)
    for p in range(W_ATTN // LANES):
        sl = slice(p * LANES, (p + 1) * LANES)
        qa_ref[0, :, sl] = _rope(zq[:, sl], cos, s1, s2).astype(BF16)
    zkv = proj(Z_KV, 4 * LANES)
    for p in range(4):
        sl = slice(p * LANES, (p + 1) * LANES)
        blk = zkv[:, sl]
        if p < 2:
            blk = _rope(blk, cos, s1, s2)
        kv_ref[0, :, sl] = blk.astype(BF16)
    us_ref[0] = proj(Z_US, W_S5)
    zm = proj(Z_QKVM, 3 * W_ML)
    qkvm_ref[0, :, 0:W_ML] = zm[:, 0:W_ML].astype(BF16)
    qkvm_ref[0, :, W_ML:2 * W_ML] = (zm[:, W_ML:2 * W_ML] * (ML_HEAD_DIM ** -0.5)).astype(BF16)
    qkvm_ref[0, :, 2 * W_ML:3 * W_ML] = zm[:, 2 * W_ML:3 * W_ML].astype(BF16)
    om_ref[0] = proj(Z_OM, W_ML)
    for p in range(N_BRANCH):
        gb_ref[0, :, p * d:(p + 1) * d] = proj(Z_GB + p * d, d)
    gm_ref[0] = proj(Z_GB + N_BRANCH * d, LANES)
    gmt_ref[0] = _dot_nt(wgt_ref[...], h)


def _in_call(xc, modsel, norm_g, w_in, w_gt, cos, s1, s2):
    b, tc, d = xc.shape
    nz = w_in.shape[1]
    nt = tc // ROW_TILE

    def row(bi, i):
        return (bi, i, 0)

    out_shapes = (
        jax.ShapeDtypeStruct((b, tc, W_ATTN), BF16),
        jax.ShapeDtypeStruct((b, tc, 4 * LANES), BF16),
        jax.ShapeDtypeStruct((b, tc, W_S5), F32),
        jax.ShapeDtypeStruct((b, tc, 3 * W_ML), BF16),
        jax.ShapeDtypeStruct((b, tc, W_ML), F32),
        jax.ShapeDtypeStruct((b, tc, N_BRANCH * d), F32),
        jax.ShapeDtypeStruct((b, tc, LANES), F32),
        jax.ShapeDtypeStruct((b, N_GATES, tc), F32),
    )
    out_specs = [pl.BlockSpec((1, ROW_TILE, s.shape[2]), row) for s in out_shapes[:-1]]
    out_specs.append(pl.BlockSpec((1, N_GATES, ROW_TILE), lambda bi, i: (bi, 0, i)))
    return pl.pallas_call(
        functools.partial(_in_kernel, d_model=d),
        grid=(b, nt),
        in_specs=[
            pl.BlockSpec((1, ROW_TILE, d), row),
            pl.BlockSpec((1, 1, 6 * d), lambda bi, i: (jnp.where(i >= 1, b + bi, bi), 0, 0)),
            pl.BlockSpec((1, d), lambda bi, i: (0, 0)),
            pl.BlockSpec((d, nz), lambda bi, i: (0, 0)),
            pl.BlockSpec((N_GATES, d), lambda bi, i: (0, 0)),
            pl.BlockSpec((ROW_TILE, LANES), lambda bi, i: (i, 0)),
            pl.BlockSpec((ROW_TILE, LANES), lambda bi, i: (i, 0)),
            pl.BlockSpec((ROW_TILE, LANES), lambda bi, i: (i, 0)),
        ],
        out_specs=out_specs,
        out_shape=out_shapes,
        compiler_params=_params(("parallel", "parallel"), V7X_VMEM_LIMIT_BYTES),
        name="in_proj",
    )(xc, modsel, norm_g.reshape(1, d), w_in, w_gt, cos, s1, s2)


def _attn_kernel(sink_ref, q_ref, kvp_ref, kvm_ref, kvn_ref, kvc_ref, o_ref, kv_scr, *, n_tok):
    i = pl.program_id(1)
    n_loc = ROW_TILE + 2 * WINDOW
    n_key = n_loc + ROW_TILE
    kv_scr[0:WINDOW] = kvp_ref[0]
    kv_scr[WINDOW:WINDOW + ROW_TILE] = kvm_ref[0]
    kv_scr[WINDOW + ROW_TILE:n_loc] = kvn_ref[0]
    kv_scr[n_loc:n_key] = kvc_ref[0]

    t0 = (i - 1) * ROW_TILE
    row = lax.broadcasted_iota(jnp.int32, (ROW_TILE, n_key), 0)
    col = lax.broadcasted_iota(jnp.int32, (ROW_TILE, n_key), 1)
    lo = jnp.maximum(row, WINDOW - t0)
    hi = jnp.minimum(jnp.minimum(row + 2 * WINDOW, n_tok - t0 + WINDOW - 1), n_loc - 1)
    hi = jnp.where(i >= 1, hi, -1)
    valid = jnp.logical_or(jnp.logical_and(col >= lo, col <= hi), col >= n_loc)
    bias = jnp.where(valid, 0.0, NEG_BIG)

    lane = lax.broadcasted_iota(jnp.int32, (n_key, LANES), 1)
    scale = HEAD_DIM ** -0.5
    for p in range(N_HEADS // 2):
        kvh = p // (Q_PER_KV // 2)
        kd = kv_scr[:, kvh * LANES:(kvh + 1) * LANES]
        vd = kv_scr[:, (N_KV_HEADS + kvh) * LANES:(N_KV_HEADS + kvh + 1) * LANES]
        qb = q_ref[0, :, p * LANES:(p + 1) * LANES]
        acc = jnp.zeros((ROW_TILE, LANES), F32)
        for par in range(2):
            head = 2 * p + par
            half = (lane >= HEAD_DIM) if par else (lane < HEAD_DIM)
            km = jnp.where(half, kd, jnp.zeros_like(kd))
            vm = jnp.where(half, vd, jnp.zeros_like(vd))
            s = _dot_nt(qb, km) * scale + bias
            sk = sink_ref[head]
            m = jnp.maximum(jnp.max(s, axis=1, keepdims=True), sk)
            e = jnp.exp(s - m)
            den = jnp.sum(e, axis=1, keepdims=True) + jnp.exp(sk - m)
            acc = acc + _dot(e.astype(BF16), vm) * (1.0 / den)
        o_ref[0, :, p * LANES:(p + 1) * LANES] = acc.astype(BF16)


def _attn_call(sink, qa, kv, n_tok):
    b, tc, _ = qa.shape
    nt = tc // ROW_TILE
    n_half = tc // WINDOW
    per = ROW_TILE // WINDOW
    kvw = kv.shape[2]
    return pl.pallas_call(
        functools.partial(_attn_kernel, n_tok=n_tok),
        grid=(b, nt),
        in_specs=[
            pl.BlockSpec(memory_space=pltpu.SMEM),
            pl.BlockSpec((1, ROW_TILE, W_ATTN), lambda bi, i: (bi, i, 0)),
            pl.BlockSpec((1, WINDOW, kvw), lambda bi, i: (bi, jnp.maximum(per * i - 1, 0), 0)),
            pl.BlockSpec((1, ROW_TILE, kvw), lambda bi, i: (bi, i, 0)),
            pl.BlockSpec((1, WINDOW, kvw), lambda bi, i: (bi, jnp.minimum(per * i + per, n_half - 1), 0)),
            pl.BlockSpec((1, ROW_TILE, kvw), lambda bi, i: (bi, 0, 0)),
        ],
        out_specs=pl.BlockSpec((1, ROW_TILE, W_ATTN), lambda bi, i: (bi, i, 0)),
        out_shape=jax.ShapeDtypeStruct((b, tc, W_ATTN), BF16),
        scratch_shapes=[pltpu.VMEM((2 * ROW_TILE + 2 * WINDOW, kvw), BF16)],
        compiler_params=_params(("parallel", "parallel"), V7X_VMEM_LIMIT_BYTES),
        name="attention",
    )(sink, qa, kv, kv, kv, kv)


def _s5_scan_tile(u_ref, bblk_ref, cblk_ref, lam_ref, x_scr, st_scr, *, reverse, n_batch):
    i = pl.program_id(0)

    @pl.when(i == 0)
    def _():
        st_scr[...] = jnp.zeros_like(st_scr)

    u = u_ref[...]
    x_scr[...] = _dot(u.astype(BF16), bblk_ref[...])
    lre = jnp.broadcast_to(lam_ref[0:1, :], (n_batch, S5_MODES))
    lim = jnp.broadcast_to(lam_ref[1:2, :], (n_batch, S5_MODES))

    def step(j, carry):
        re, im = carry
        t = (S5_TILE - 1 - j) if reverse else j
        r0 = pl.multiple_of(t * n_batch, n_batch)
        bre = x_scr[pl.ds(r0, n_batch), 0:S5_MODES]
        bim = x_scr[pl.ds(r0, n_batch), S5_MODES:2 * S5_MODES]
        nre = lre * re - lim * im + bre
        nim = lre * im + lim * re + bim
        x_scr[pl.ds(r0, n_batch), 0:S5_MODES] = nre
        x_scr[pl.ds(r0, n_batch), S5_MODES:2 * S5_MODES] = nim
        return nre, nim

    re, im = lax.fori_loop(0, S5_TILE, step, (st_scr[0], st_scr[1]), unroll=8)
    st_scr[0] = re
    st_scr[1] = im
    return u, _dot(x_scr[...].astype(BF16), cblk_ref[...])


def _s5_fwd_kernel(u_ref, bblk_ref, cblk_ref, lam_ref, o_ref, x_scr, st_scr, *, n_batch):
    _, y = _s5_scan_tile(u_ref, bblk_ref, cblk_ref, lam_ref, x_scr, st_scr, reverse=False, n_batch=n_batch)
    o_ref[...] = y


def _s5_bwd_kernel(u_ref, bblk_ref, cblk_ref, lam_ref, yf_ref, dsk_ref, wg_ref, bg_ref, o_ref, x_scr, st_scr,
                   *, n_batch):
    u, y = _s5_scan_tile(u_ref, bblk_ref, cblk_ref, lam_ref, x_scr, st_scr, reverse=True, n_batch=n_batch)
    y = _gelu_tanh(y + yf_ref[...] + dsk_ref[...] * u)
    o_ref[...] = (y * _sigmoid(_dot(y.astype(BF16), wg_ref[...]) + bg_ref[...])).astype(BF16)


def _s5_calls(u_rows, prm, n_batch, n_ctx):
    rows_total = u_rows.shape[0]
    tile_rows = S5_TILE * n_batch
    nt = rows_total // tile_rows
    nct = n_ctx // S5_TILE

    def fwd_tile(i):
        return (i, 0)

    def bwd_tile(i):
        return (jnp.where(i < nct, nct - 1 - i, nt - 1 - (i - nct)), 0)

    def const(i):
        return (0, 0)

    scratch = [pltpu.VMEM((tile_rows, 2 * S5_MODES), F32), pltpu.VMEM((2, n_batch, S5_MODES), F32)]
    wspecs = [
        pl.BlockSpec((W_S5, 2 * S5_MODES), const),
        pl.BlockSpec((2 * S5_MODES, W_S5), const),
        pl.BlockSpec((2, S5_MODES), const),
    ]
    yf = pl.pallas_call(
        functools.partial(_s5_fwd_kernel, n_batch=n_batch),
        grid=(nt,),
        in_specs=[pl.BlockSpec((tile_rows, W_S5), fwd_tile)] + wspecs,
        out_specs=pl.BlockSpec((tile_rows, W_S5), fwd_tile),
        out_shape=jax.ShapeDtypeStruct((rows_total, W_S5), F32),
        scratch_shapes=scratch,
        compiler_params=_params(("arbitrary",), V7X_VMEM_LIMIT_BYTES),
        name="s5_forward",
    )(u_rows, prm["bblk"][0], prm["cblk"], prm["lam"][0])
    return pl.pallas_call(
        functools.partial(_s5_bwd_kernel, n_batch=n_batch),
        grid=(nt,),
        in_specs=[pl.BlockSpec((tile_rows, W_S5), bwd_tile)] + wspecs + [
            pl.BlockSpec((tile_rows, W_S5), bwd_tile),
            pl.BlockSpec((1, W_S5), const),
            pl.BlockSpec((W_S5, W_S5), const),
            pl.BlockSpec((1, W_S5), const),
        ],
        out_specs=pl.BlockSpec((tile_rows, W_S5), bwd_tile),
        out_shape=jax.ShapeDtypeStruct((rows_total, W_S5), BF16),
        scratch_shapes=scratch,
        compiler_params=_params(("arbitrary",), V7X_VMEM_LIMIT_BYTES),
        name="s5_backward",
    )(u_rows, prm["bblk"][1], prm["cblk"], prm["lam"][1], yf, prm["dsk"], prm["wglu"], prm["bglu"])


def _s5_prep(a_re, a_im, log_dt, b_re, b_im, c_re, c_im, d_skip, glu_w, glu_b):
    eye = jnp.eye(S5_GROUPS, dtype=F32)
    bblk, lam = [], []
    for d in range(2):
        dt = jnp.exp(log_dt[d])[:, None]
        dre, dim = dt * a_re[d], dt * a_im[d]
        mag = jnp.exp(dre)
        lam_re, lam_im = mag * jnp.cos(dim), mag * jnp.sin(dim)
        den = a_re[d] * a_re[d] + a_im[d] * a_im[d]
        coef_re = ((lam_re - 1.0) * a_re[d] + lam_im * a_im[d]) / den
        coef_im = (lam_im * a_re[d] - (lam_re - 1.0) * a_im[d]) / den
        cr, ci = coef_re[..., None], coef_im[..., None]
        bb_re = cr * b_re - ci * b_im
        bb_im = cr * b_im + ci * b_re
        blk = [jnp.einsum("gpc,gh->gchp", bb, eye).reshape(W_S5, S5_MODES) for bb in (bb_re, bb_im)]
        bblk.append(jnp.concatenate(blk, axis=1).astype(BF16))
        lam.append(jnp.stack([lam_re.reshape(-1), lam_im.reshape(-1)]))
    cblk = [jnp.einsum("gcp,gh->gphc", cc, eye).reshape(S5_MODES, W_S5) for cc in (c_re, -c_im)]
    return dict(bblk=bblk, lam=lam, cblk=jnp.concatenate(cblk, axis=0).astype(BF16),
                dsk=d_skip.reshape(1, W_S5), wglu=glu_w.astype(BF16), bglu=glu_b.reshape(1, W_S5))


def _head_blocks(cols):
    shape = (cols[0].shape[0], LANES)
    first = lax.broadcasted_iota(jnp.int32, shape, 1) < ML_HEAD_DIM
    return jnp.concatenate([jnp.where(first, cols[0], cols[1]), jnp.where(first, cols[2], cols[3])], axis=1)


def _mlstm_chunk(qkv_ref, gm_ref, gmt_ref, om_ref, brow_ref, bcol_ref, ng_ref, o_ref,
                 hf_scr, ct_scr, n_scr, mc_scr, mr_scr, *, reverse, row0):
    n = ML_CHUNK
    g0 = 2 * ML_HEADS * (1 if reverse else 0)
    last = 0 if reverse else n - 1
    q = qkv_ref[0, :, 0:W_ML]
    k = qkv_ref[0, :, W_ML:2 * W_ML]
    v = qkv_ref[0, :, 2 * W_ML:3 * W_ML]
    ti = lax.broadcasted_iota(jnp.int32, (n, n), 0)
    si = lax.broadcasted_iota(jnp.int32, (n, n), 1)
    causal = (si >= ti) if reverse else (si <= ti)
    tri = jnp.where(causal, 1.0, 0.0)
    tri_t = jnp.where((ti >= si) if reverse else (ti <= si), 1.0, 0.0)

    g_col = gm_ref[0] + brow_ref[...]
    lf_col = _log_sigmoid(pltpu.roll(g_col, LANES - ML_HEADS, 1))
    bcum = _dot_exact(tri, lf_col)
    r_col = g_col - bcum
    rowi = lax.broadcasted_iota(jnp.int32, (n, LANES), 0)
    cmax = r_col
    sh = 1
    while sh < n:
        if reverse:
            cmax = jnp.where(rowi < n - sh, jnp.maximum(cmax, pltpu.roll(cmax, n - sh, 0)), cmax)
        else:
            cmax = jnp.where(rowi >= sh, jnp.maximum(cmax, pltpu.roll(cmax, sh, 0)), cmax)
        sh *= 2
    m_col = mc_scr[...]
    mm = jnp.maximum(m_col, cmax)
    w_inter = jnp.exp(m_col - mm)
    floor = jnp.exp(-(bcum + mm))
    mm_last = mm[last:last + 1, :]
    w_state = jnp.exp(r_col - mm_last)
    mc_scr[...] = bcum[last:last + 1, :] + mm_last

    g_row = gmt_ref[0, g0:g0 + 2 * ML_HEADS, :] + bcol_ref[g0:g0 + 2 * ML_HEADS, :]
    lf_row = _log_sigmoid(g_row)
    bcum_row = _dot_exact(lf_row, tri_t)
    r_row = g_row[0:ML_HEADS] - bcum_row[ML_HEADS:2 * ML_HEADS]
    m_row = mr_scr[0:ML_HEADS, :]
    mm_row = jnp.maximum(m_row, jnp.max(r_row, axis=1, keepdims=True))
    decay_row = jnp.exp(m_row - mm_row)
    mr_scr[0:ML_HEADS, :] = jnp.sum(lf_row[ML_HEADS:2 * ML_HEADS], axis=1, keepdims=True) + mm_row
    decay_col = jnp.concatenate(
        [jnp.broadcast_to(decay_row[h:h + 1, :], (ML_HEAD_DIM, LANES)) for h in range(ML_HEADS)], axis=0)

    lane_head = lax.broadcasted_iota(jnp.int32, (n, W_ML), 1) // ML_HEAD_DIM
    zero_b = jnp.zeros((n, W_ML), BF16)
    kexp = jnp.concatenate([jnp.where(lane_head == h, k, zero_b) for h in range(ML_HEADS)], axis=0)
    vexp = jnp.concatenate([jnp.where(lane_head == h, v, zero_b) for h in range(ML_HEADS)], axis=0)
    s_all = _dot_nt(q, kexp)
    w_list, rowsum = [], []
    for h in range(ML_HEADS):
        mm_b = jnp.broadcast_to(mm[:, g0 + h:g0 + h + 1], (n, n))
        arg = jnp.where(causal, r_row[h:h + 1, :] - mm_b, NEG_BIG)
        w = jnp.exp(arg) * s_all[:, h * n:(h + 1) * n]
        rowsum.append(jnp.sum(w, axis=1, keepdims=True))
        w_list.append(w.astype(BF16))
    num_intra = _dot(jnp.concatenate(w_list, axis=1), vexp)

    ct = ct_scr[...]
    nst = n_scr[...]
    bi = lax.broadcasted_iota(jnp.int32, (W_ML, W_ML), 0) // ML_HEAD_DIM
    bj = lax.broadcasted_iota(jnp.int32, (W_ML, W_ML), 1) // ML_HEAD_DIM
    same_head = bi == bj
    nblk = jnp.where(same_head, jnp.concatenate([nst, nst], axis=1), 0.0)
    q_c = _dot(q, ct.astype(BF16))
    q_n = _dot(q, nblk.astype(BF16))

    def per_head(x):
        return _head_blocks([x[:, g0 + h:g0 + h + 1] for h in range(ML_HEADS)])

    wi_b = per_head(w_inter)
    num = wi_b * q_c + num_intra
    den = wi_b * q_n + _head_blocks(rowsum)
    h_out = num / jnp.maximum(jnp.abs(den), per_head(floor))

    kw = (k.astype(F32) * per_head(w_state)).astype(BF16)
    vext = jnp.concatenate([v, jnp.ones((n, LANES), BF16)], axis=1)
    upd = _dot_tn(kw, vext)
    ct_scr[...] = jnp.concatenate([decay_col, decay_col], axis=1) * ct + jnp.where(same_head, upd[:, 0:W_ML], 0.0)
    n_scr[...] = decay_col * nst + upd[:, W_ML:W_ML + LANES]

    if not reverse:
        hf_scr[pl.ds(row0, n), :] = h_out
    else:
        ht = h_out + hf_scr[pl.ds(row0, n), :]
        sq = ht * ht
        ms = [jnp.sum(jnp.where(lane_head == h, sq, 0.0), axis=1, keepdims=True) * (1.0 / ML_HEAD_DIM)
              for h in range(ML_HEADS)]
        hn = ht * lax.rsqrt(_head_blocks(ms) + NORM_EPS) * ng_ref[...]
        o_ref[0] = (hn * _sigmoid(om_ref[0])).astype(BF16)


def _ml_tile(s, nt, nct):
    sp = s - nt
    back = jnp.where(sp < nct, nct - 1 - sp, nt - 1 - (sp - nct))
    return jnp.where(s < nt, s, back)


def _mlstm_kernel(qkv_ref, gm_ref, gmt_ref, om_ref, brow_ref, bcol_ref, ng_ref, o_ref,
                  hf_scr, ct_scr, n_scr, mc_scr, mr_scr, *, nt, nct):
    s = pl.program_id(1)

    @pl.when(jnp.logical_or(s == 0, s == nt))
    def _():
        ct_scr[...] = jnp.zeros_like(ct_scr)
        n_scr[...] = jnp.zeros_like(n_scr)
        mc_scr[...] = jnp.zeros_like(mc_scr)
        mr_scr[...] = jnp.zeros_like(mr_scr)

    row0 = pl.multiple_of(_ml_tile(s, nt, nct) * ML_CHUNK, ML_CHUNK)
    refs = (qkv_ref, gm_ref, gmt_ref, om_ref, brow_ref, bcol_ref, ng_ref, o_ref,
            hf_scr, ct_scr, n_scr, mc_scr, mr_scr)

    @pl.when(s < nt)
    def _():
        _mlstm_chunk(*refs, reverse=False, row0=row0)

    @pl.when(s >= nt)
    def _():
        _mlstm_chunk(*refs, reverse=True, row0=row0)


def _mlstm_call(qkvm, gm, gmt, om, bias_row, bias_col, norm_g, n_ctx):
    b, tc, _ = qkvm.shape
    nt = tc // ML_CHUNK
    nct = n_ctx // ML_CHUNK

    def tile(bi, s):
        return (bi, _ml_tile(s, nt, nct), 0)

    def out_tile(bi, s):
        return (bi, _ml_tile(jnp.maximum(s, nt), nt, nct), 0)

    def const(bi, s):
        return (0, 0)

    return pl.pallas_call(
        functools.partial(_mlstm_kernel, nt=nt, nct=nct),
        grid=(b, 2 * nt),
        in_specs=[
            pl.BlockSpec((1, ML_CHUNK, 3 * W_ML), tile),
            pl.BlockSpec((1, ML_CHUNK, LANES), tile),
            pl.BlockSpec((1, N_GATES, ML_CHUNK), lambda bi, s: (bi, 0, _ml_tile(s, nt, nct))),
            pl.BlockSpec((1, ML_CHUNK, W_ML), tile),
            pl.BlockSpec((1, LANES), const),
            pl.BlockSpec((N_GATES, LANES), const),
            pl.BlockSpec((1, W_ML), const),
        ],
        out_specs=pl.BlockSpec((1, ML_CHUNK, W_ML), out_tile),
        out_shape=jax.ShapeDtypeStruct((b, tc, W_ML), BF16),
        scratch_shapes=[
            pltpu.VMEM((tc, W_ML), F32),
            pltpu.VMEM((W_ML, W_ML), F32),
            pltpu.VMEM((W_ML, LANES), F32),
            pltpu.VMEM((1, LANES), F32),
            pltpu.VMEM((8, LANES), F32),
        ],
        compiler_params=_params(("parallel", "arbitrary"), V7X_VMEM_LIMIT_BYTES),
        name="mlstm",
    )(qkvm, gm, gmt, om, bias_row, bias_col, norm_g.reshape(1, W_ML))


def _merge_kernel(ya_ref, ys_ref, ym_ref, gb_ref, x_ref, mod_ref, g_ref, wa_ref, ws_ref, wm_ref, wo_ref,
                  x1_ref, h2_ref, *, d_model):
    d = d_model
    mod = mod_ref[0]
    y = (_sigmoid(gb_ref[0, :, 0:d]) * _dot(ya_ref[0], wa_ref[...])
         + _sigmoid(gb_ref[0, :, d:2 * d]) * _dot(ys_ref[0], ws_ref[...])
         + _sigmoid(gb_ref[0, :, 2 * d:3 * d]) * _dot(ym_ref[0], wm_ref[...]))
    x1 = x_ref[0] + mod[:, 2 * d:3 * d] * _dot(y.astype(BF16), wo_ref[...])
    x1_ref[0] = x1
    h2_ref[0] = (_rms(x1) * g_ref[...] * (1.0 + mod[:, 4 * d:5 * d]) + mod[:, 3 * d:4 * d]).astype(BF16)


def _merge_call(ya, ys, ym, gb, xc, modsel, norm_g, wa, ws, wm, wo):
    b, tc, d = xc.shape
    nt = tc // ROW_TILE

    def row(bi, i):
        return (bi, i, 0)

    def const(bi, i):
        return (0, 0)

    return pl.pallas_call(
        functools.partial(_merge_kernel, d_model=d),
        grid=(b, nt),
        in_specs=[
            pl.BlockSpec((1, ROW_TILE, W_ATTN), row),
            pl.BlockSpec((1, ROW_TILE, W_S5), row),
            pl.BlockSpec((1, ROW_TILE, W_ML), row),
            pl.BlockSpec((1, ROW_TILE, N_BRANCH * d), row),
            pl.BlockSpec((1, ROW_TILE, d), row),
            pl.BlockSpec((1, 1, 6 * d), lambda bi, i: (jnp.where(i >= 1, b + bi, bi), 0, 0)),
            pl.BlockSpec((1, d), const),
            pl.BlockSpec((W_ATTN, d), const),
            pl.BlockSpec((W_S5, d), const),
            pl.BlockSpec((W_ML, d), const),
            pl.BlockSpec((d, d), const),
        ],
        out_specs=[pl.BlockSpec((1, ROW_TILE, d), row), pl.BlockSpec((1, ROW_TILE, d), row)],
        out_shape=(jax.ShapeDtypeStruct((b, tc, d), F32), jax.ShapeDtypeStruct((b, tc, d), BF16)),
        compiler_params=_params(("parallel", "parallel"), V7X_VMEM_LIMIT_BYTES),
        name="merge",
    )(ya, ys, ym, gb, xc, modsel, norm_g.reshape(1, d), wa, ws, wm, wo)


def _ffn_kernel(hp_ref, hm_ref, hn_ref, x1_ref, mod_ref, wup_ref, cw_ref, cb_ref, wdn_ref, fg_ref, o_ref,
                lhs_scr, u_scr, *, d_model, d_ff, tile_off, n_tiles, final):
    d = d_model
    halo = BF16_SUBLANES
    i = pl.program_id(1) + tile_off
    prev_ok = i >= 2
    next_ok = jnp.logical_and(i >= 1, i < n_tiles - 1)
    lhs_scr[0:halo] = jnp.where(prev_ok, hp_ref[0], jnp.zeros_like(hp_ref[0]))
    lhs_scr[halo:halo + ROW_TILE] = hm_ref[0]
    lhs_scr[halo + ROW_TILE:2 * halo + ROW_TILE] = jnp.where(next_ok, hn_ref[0], jnp.zeros_like(hn_ref[0]))
    lhs = lhs_scr[...]
    acc = jnp.zeros((ROW_TILE, d), F32)
    for j in range(d_ff // FFN_CHUNK):
        parts = []
        for lo in (j * FFN_CHUNK, d_ff + j * FFN_CHUNK):
            u_scr[...] = _dot(lhs, wup_ref[:, lo:lo + FFN_CHUNK])
            cw = cw_ref[:, lo:lo + FFN_CHUNK]
            parts.append(u_scr[halo - 1:halo - 1 + ROW_TILE, :] * cw[0:1]
                         + u_scr[halo:halo + ROW_TILE, :] * cw[1:2]
                         + u_scr[halo + 1:halo + 1 + ROW_TILE, :] * cw[2:3]
                         + cb_ref[:, lo:lo + FFN_CHUNK])
        a, v = parts
        act = (a * _sigmoid(a) * v).astype(BF16)
        acc = acc + _dot(act, wdn_ref[j * FFN_CHUNK:(j + 1) * FFN_CHUNK, :])
    out = x1_ref[0] + mod_ref[0][:, 5 * d:6 * d] * acc
    if final:
        out = _rms(out) * fg_ref[...]
    o_ref[0] = out


def _ffn_call(h2, x1, modsel, w_up, conv_w, conv_b, w_down, final_g, *, final):
    b, tc, d = x1.shape
    d_ff = w_down.shape[0]
    n_tiles = tc // ROW_TILE
    tile_off = 1 if final else 0
    nt = n_tiles - tile_off
    halo = BF16_SUBLANES
    per = ROW_TILE // halo
    n_halo = tc // halo

    def const(bi, i):
        return (0, 0)

    return pl.pallas_call(
        functools.partial(_ffn_kernel, d_model=d, d_ff=d_ff, tile_off=tile_off, n_tiles=n_tiles, final=final),
        grid=(b, nt),
        in_specs=[
            pl.BlockSpec((1, halo, d), lambda bi, i: (bi, jnp.maximum((i + tile_off) * per - 1, 0), 0)),
            pl.BlockSpec((1, ROW_TILE, d), lambda bi, i: (bi, i + tile_off, 0)),
            pl.BlockSpec((1, halo, d), lambda bi, i: (bi, jnp.minimum((i + tile_off + 1) * per, n_halo - 1), 0)),
            pl.BlockSpec((1, ROW_TILE, d), lambda bi, i: (bi, i + tile_off, 0)),
            pl.BlockSpec((1, 1, 6 * d), lambda bi, i: (jnp.where(i + tile_off >= 1, b + bi, bi), 0, 0)),
            pl.BlockSpec((d, 2 * d_ff), const),
            pl.BlockSpec((3, 2 * d_ff), const),
            pl.BlockSpec((1, 2 * d_ff), const),
            pl.BlockSpec((d_ff, d), const),
            pl.BlockSpec((1, d), const),
        ],
        out_specs=pl.BlockSpec((1, ROW_TILE, d), lambda bi, i: (bi, i, 0)),
        out_shape=jax.ShapeDtypeStruct((b, nt * ROW_TILE, d), F32),
        scratch_shapes=[pltpu.VMEM((ROW_TILE + 2 * halo, d), BF16), pltpu.VMEM((ROW_TILE + 2 * halo, FFN_CHUNK), F32)],
        compiler_params=_params(("parallel", "parallel"), V7X_VMEM_LIMIT_BYTES),
        name="conv_ffn",
    )(h2, h2, h2, x1, modsel, w_up, conv_w, conv_b.reshape(1, 2 * d_ff), w_down, final_g.reshape(1, d))


def _prep_w_in(w):
    d = w.shape[0]
    k0, k1 = w[:, W_ATTN:W_ATTN + HEAD_DIM], w[:, W_ATTN + HEAD_DIM:W_ATTN + 2 * HEAD_DIM]
    o = W_ATTN + 2 * HEAD_DIM
    v0, v1 = w[:, o:o + HEAD_DIM], w[:, o + HEAD_DIM:o + 2 * HEAD_DIM]
    o += 2 * HEAD_DIM
    rest = w[:, o:o + W_S5 + 4 * W_ML]
    o += W_S5 + 4 * W_ML
    gm = w[:, o:o + N_GATES]
    gb = w[:, o + N_GATES:o + N_GATES + N_BRANCH * d]
    cols = [w[:, 0:W_ATTN], k0, k0, k1, k1, v0, v0, v1, v1, rest, gb, jnp.tile(gm, (1, LANES // N_GATES))]
    return jnp.concatenate(cols, axis=1).astype(BF16), gm.T.astype(BF16)


def _rope_tables(n_ctx, n_tok):
    pos = jnp.arange(n_tok)
    row = (pos // GRID_W).astype(F32)
    col = (pos % GRID_W).astype(F32)
    n_freq = HEAD_DIM // 4
    inv_freq = ROPE_BASE ** (-jnp.arange(n_freq, dtype=F32) / n_freq)
    lane = jnp.arange(LANES)
    dd = lane % HEAD_DIM
    by_col = (dd // (HEAD_DIM // 2)) == 1
    d32 = dd % (HEAD_DIM // 2)
    first = d32 < n_freq
    freq = inv_freq[d32 % n_freq]
    ang = jnp.where(by_col[None, :], col[:, None], row[:, None]) * freq[None, :]
    cos, sin = jnp.cos(ang), jnp.sin(ang)
    s1 = jnp.where(first[None, :], -sin, 0.0)
    s2 = jnp.where(first[None, :], 0.0, sin)
    ones = jnp.ones((n_ctx, LANES), F32)
    zeros = jnp.zeros((n_ctx, LANES), F32)
    return (jnp.concatenate([ones, cos]), jnp.concatenate([zeros, s1]), jnp.concatenate([zeros, s2]))


def kernel(x, c, ctx, c_ctx, mod_w, mod_b, norm1_g, norm2_g, w_in, attn_sink, s5_a_re, s5_a_im, s5_log_dt, s5_b_re, s5_b_im, s5_c_re, s5_c_im, s5_d, s5_glu_w, s5_glu_b, ml_igate_b, ml_fgate_b, ml_norm_g, w_branch_attn, w_branch_s5, w_branch_ml, w_out, ffn_w_up, ffn_conv_w, ffn_conv_b, ffn_w_down, final_norm_g):
    b, n_tok, d = x.shape
    n_ctx = ctx.shape[1]
    n_layer = mod_w.shape[0]
    assert n_ctx == ROW_TILE and n_tok % ROW_TILE == 0 and n_tok % GRID_W == 0
    assert b % 8 == 0 and b + 1 <= 2 * 8 and d % LANES == 0
    tc = n_ctx + n_tok

    xc = jnp.concatenate([ctx, x], axis=1)
    c_all = jnp.zeros((16, d), F32).at[:b].set(c).at[b].set(c_ctx)
    mod = _mod_call(c_all, mod_w.astype(BF16), mod_b)
    cos, s1, s2 = _rope_tables(n_ctx, n_tok)

    for l in range(n_layer):
        final = l == n_layer - 1
        modsel = jnp.concatenate([jnp.broadcast_to(mod[l, b:b + 1], (b, 6 * d)), mod[l, :b]]).reshape(2 * b, 1, 6 * d)
        w_in_l, w_gt = _prep_w_in(w_in[l])
        qa, kv, us, qkvm, om, gb, gm, gmt = _in_call(xc, modsel, norm1_g[l], w_in_l, w_gt, cos, s1, s2)

        ya = _attn_call(attn_sink[l], qa, kv, n_tok)

        s5p = _s5_prep(s5_a_re[l], s5_a_im[l], s5_log_dt[l], s5_b_re[l], s5_b_im[l], s5_c_re[l], s5_c_im[l],
                       s5_d[l], s5_glu_w[l], s5_glu_b[l])
        u_rows = us.transpose(1, 0, 2).reshape(tc * b, W_S5)
        ys = _s5_calls(u_rows, s5p, b, n_ctx).reshape(tc, b, W_S5).transpose(1, 0, 2)

        gate_b = jnp.concatenate([ml_igate_b[l, 0], ml_fgate_b[l, 0], ml_igate_b[l, 1], ml_fgate_b[l, 1]])
        bias_row = jnp.tile(gate_b, LANES // N_GATES).reshape(1, LANES)
        bias_col = jnp.broadcast_to(gate_b[:, None], (N_GATES, LANES))
        ym = _mlstm_call(qkvm, gm, gmt, om, bias_row, bias_col, ml_norm_g[l], n_ctx)

        x1, h2 = _merge_call(ya, ys, ym, gb, xc, modsel, norm2_g[l], w_branch_attn[l].astype(BF16),
                             w_branch_s5[l].astype(BF16), w_branch_ml[l].astype(BF16), w_out[l].astype(BF16))
        xc = _ffn_call(h2, x1, modsel, ffn_w_up[l].astype(BF16), ffn_conv_w[l], ffn_conv_b[l],
                       ffn_w_down[l].astype(BF16), final_norm_g, final=final)
    return xc
```

```python
import functools
import math

import jax
import jax.numpy as jnp
from jax import lax
from jax.experimental import pallas as pl
from jax.experimental.pallas import tpu as pltpu

F32 = jnp.float32
BF16 = jnp.bfloat16

N_HEADS = 8
N_KV_HEADS = 2
HEAD_DIM = 64
Q_PER_KV = N_HEADS // N_KV_HEADS
W_ATTN = N_HEADS * HEAD_DIM
WINDOW = 128
GRID_W = 64
ROPE_BASE = 10000.0
W_S5 = 256
S5_GROUP = 16
S5_GROUPS = W_S5 // S5_GROUP
S5_STATE = 64
S5_MODES = S5_GROUPS * S5_STATE
ML_HEADS = 4
ML_HEAD_DIM = 64
W_ML = ML_HEADS * ML_HEAD_DIM
N_GATES = 4 * ML_HEADS
N_BRANCH = 3
NORM_EPS = 1e-6

LANES = 128
BF16_SUBLANES = 16
V7X_VMEM_LIMIT_BYTES = 56 * 1024 * 1024

ROW_TILE = 512
FFN_TILE = 1024
ATTN_TILE = 256
ML_CHUNK = 128
ML_BATCH = 4
S5_TILE = 128
FFN_CHUNK = 256
NEG_BIG = -1e30
LOG2E = math.log2(math.e)
Q_SCALE = LOG2E * HEAD_DIM ** -0.5

Z_QA = 0
Z_KV = Z_QA + W_ATTN
Z_US = Z_KV + 4 * LANES
Z_QKVM = Z_US + W_S5
Z_OM = Z_QKVM + 3 * W_ML
Z_GB = Z_OM + W_ML


def _sigmoid(x):
    return 1.0 / (1.0 + jnp.exp(-x))


def _log_sigmoid(x):
    return jnp.minimum(x, 0.0) - jnp.log(1.0 + jnp.exp(-jnp.abs(x)))


def _gelu_tanh(x):
    return 0.5 * x * (1.0 + jnp.tanh(math.sqrt(2.0 / math.pi) * (x + 0.044715 * (x * x * x))))


def _rms(x):
    return x * lax.rsqrt(jnp.mean(x * x, axis=-1, keepdims=True) + NORM_EPS)


def _dot(a, b):
    return jnp.dot(a, b, preferred_element_type=F32)


def _dot_nt(a, b):
    return lax.dot_general(a, b, (((1,), (1,)), ((), ())), preferred_element_type=F32)


def _dot_tn(a, b):
    return lax.dot_general(a, b, (((0,), (0,)), ((), ())), preferred_element_type=F32)


def _dot_exact(a, b):
    return jnp.dot(a, b, preferred_element_type=F32, precision=lax.Precision.HIGHEST)


def _params(sem, vmem=None):
    return pltpu.CompilerParams(dimension_semantics=sem, vmem_limit_bytes=vmem)


def _resident(shape):
    return pl.BlockSpec(shape, lambda *_: (0,) * len(shape), pipeline_mode=pl.Buffered(1))


def _mod_kernel(c_ref, w_ref, b_ref, o_ref):
    c = c_ref[...]
    s = (c * _sigmoid(c)).astype(BF16)
    o_ref[0] = _dot(s, w_ref[0]) + b_ref[0]


def _mod_call(c_all, mod_w, mod_b):
    n_layer, d, d6 = mod_w.shape
    tn = d6 // 4
    rows = c_all.shape[0]
    return pl.pallas_call(
        _mod_kernel,
        grid=(n_layer, d6 // tn),
        in_specs=[
            pl.BlockSpec((rows, d), lambda l, j: (0, 0)),
            pl.BlockSpec((1, d, tn), lambda l, j: (l, 0, j)),
            pl.BlockSpec((1, 1, tn), lambda l, j: (l, 0, j)),
        ],
        out_specs=pl.BlockSpec((1, rows, tn), lambda l, j: (l, 0, j)),
        out_shape=jax.ShapeDtypeStruct((n_layer, rows, d6), F32),
        compiler_params=_params(("parallel", "parallel")),
        name="modulation",
    )(c_all, mod_w, mod_b.reshape(n_layer, 1, d6))


def _rope(xb, cos, s1, s2):
    return xb * cos + pltpu.roll(xb, LANES - 16, 1) * s1 + pltpu.roll(xb, 16, 1) * s2


N_IN_OUT = 8


def _in_kernel(x_ref, mod_ref, g_ref, w_ref, wgt_ref, *rest, d_model, rotary):
    d = d_model
    if rotary:
        cos, s1, s2 = (r[...] for r in rest[:3])
        rest = rest[3:]
    qa_ref, kv_ref, us_ref, qkvm_ref, om_ref, gb_ref, gm_ref, gmt_ref = rest[-N_IN_OUT:]
    mod = mod_ref[0]
    h = (_rms(x_ref[0]) * g_ref[...] * (1.0 + mod[:, d:2 * d]) + mod[:, 0:d]).astype(BF16)

    def proj(lo, width):
        return _dot(h, w_ref[:, lo:lo + width])

    def rot(blk):
        return _rope(blk, cos, s1, s2) if rotary else blk

    zq = proj(Z_QA, W_ATTN)
    for p in range(W_ATTN // LANES):
        sl = slice(p * LANES, (p + 1) * LANES)
        qa_ref[0, :, sl] = (rot(zq[:, sl]) * Q_SCALE).astype(BF16)
    zkv = proj(Z_KV, 4 * LANES)
    for p in range(4):
        sl = slice(p * LANES, (p + 1) * LANES)
        kv_ref[0, :, sl] = (rot(zkv[:, sl]) if p < 2 else zkv[:, sl]).astype(BF16)
    us_ref[0] = proj(Z_US, W_S5)
    zm = proj(Z_QKVM, 3 * W_ML)
    qkvm_ref[0, :, 0:W_ML] = zm[:, 0:W_ML].astype(BF16)
    qkvm_ref[0, :, W_ML:2 * W_ML] = (zm[:, W_ML:2 * W_ML] * (ML_HEAD_DIM ** -0.5)).astype(BF16)
    qkvm_ref[0, :, 2 * W_ML:3 * W_ML] = zm[:, 2 * W_ML:3 * W_ML].astype(BF16)
    om_ref[0] = proj(Z_OM, W_ML)
    for p in range(N_BRANCH):
        gb_ref[0, :, p * d:(p + 1) * d] = proj(Z_GB + p * d, d).astype(BF16)
    gm_ref[0] = proj(Z_GB + N_BRANCH * d, LANES)
    gmt_ref[0] = _dot_nt(wgt_ref[...], h)


def _in_call(x, modsel, mod_off, norm_g, w_in, w_gt, tc, tile, blk_off, rope=None, prev=None):
    b, n, d = x.shape
    nz = w_in.shape[1]
    widths = (W_ATTN, 4 * LANES, W_S5, 3 * W_ML, W_ML, N_BRANCH * d, LANES)
    dtypes = (BF16, BF16, F32, BF16, F32, BF16, F32)
    out_shapes = [jax.ShapeDtypeStruct((b, tc, w), t) for w, t in zip(widths, dtypes)]
    out_shapes.append(jax.ShapeDtypeStruct((b, N_GATES, tc), F32))
    out_specs = [pl.BlockSpec((1, tile, w), lambda bi, i: (bi, i + blk_off, 0)) for w in widths]
    out_specs.append(pl.BlockSpec((1, N_GATES, tile), lambda bi, i: (bi, 0, i + blk_off)))
    in_specs = [
        pl.BlockSpec((1, tile, d), lambda bi, i: (bi, i, 0)),
        pl.BlockSpec((1, 1, 6 * d), lambda bi, i: (mod_off + bi, 0, 0)),
        _resident((1, d)),
        _resident((d, nz)),
        _resident((N_GATES, d)),
    ]
    args = [x, modsel, norm_g.reshape(1, d), w_in, w_gt]
    if rope is not None:
        in_specs += [pl.BlockSpec((tile, LANES), lambda bi, i: (i, 0))] * 3
        args += list(rope)
    aliases = {}
    if prev is not None:
        aliases = {len(args) + k: k for k in range(N_IN_OUT)}
        in_specs += [pl.BlockSpec(memory_space=pl.ANY)] * N_IN_OUT
        args += list(prev)
    return pl.pallas_call(
        functools.partial(_in_kernel, d_model=d, rotary=rope is not None),
        grid=(b, n // tile),
        in_specs=in_specs,
        out_specs=out_specs,
        out_shape=out_shapes,
        input_output_aliases=aliases,
        compiler_params=_params(("parallel", "parallel"), V7X_VMEM_LIMIT_BYTES),
        name="in_proj",
    )(*args)


def _attn_heads(sink_ref, q_ref, kv_scr, o_ref, bias, n_key):
    lane = lax.broadcasted_iota(jnp.int32, (n_key, LANES), 1)
    n_q = q_ref.shape[1]
    own = lax.broadcasted_iota(jnp.int32, (n_q, LANES), 1) < HEAD_DIM
    n_bias = 0 if bias is None else bias.shape[1]
    for p in range(N_HEADS // 2):
        kvh = p // (Q_PER_KV // 2)
        kd = kv_scr[0:n_key, kvh * LANES:(kvh + 1) * LANES]
        vd = kv_scr[0:n_key, (N_KV_HEADS + kvh) * LANES:(N_KV_HEADS + kvh + 1) * LANES]
        qb = q_ref[0, :, p * LANES:(p + 1) * LANES]
        outs = []
        for par in range(2):
            head = 2 * p + par
            half = (lane >= HEAD_DIM) if par else (lane < HEAD_DIM)
            km = jnp.where(half, kd, jnp.zeros_like(kd))
            vm = jnp.where(half, vd, jnp.ones_like(vd))
            s = _dot_nt(qb, km)
            parts = [s[:, 0:n_bias] + bias, s[:, n_bias:n_key]] if n_bias else [s]
            sk = sink_ref[head] * LOG2E
            m = sk
            for part in parts:
                m = jnp.maximum(m, jnp.max(part, axis=1, keepdims=True))
            e = jnp.concatenate([jnp.exp2(part - m).astype(BF16) for part in parts], axis=1)
            o = _dot(e, vm)
            den = pltpu.roll(o, HEAD_DIM, 1) + jnp.exp2(sk - m)
            outs.append(o / den)
        o_ref[0, :, p * LANES:(p + 1) * LANES] = jnp.where(own, outs[0], outs[1]).astype(BF16)


def _attn_latent_kernel(sink_ref, q_ref, kvp_ref, kvm_ref, kvn_ref, kvc_ref, o_ref, kv_scr, *, n_tok, n_ctx):
    i = pl.program_id(1)
    n_loc = ATTN_TILE + 2 * WINDOW
    n_key = n_loc + n_ctx
    kv_scr[0:WINDOW] = kvp_ref[0]
    kv_scr[WINDOW:WINDOW + ATTN_TILE] = kvm_ref[0]
    kv_scr[WINDOW + ATTN_TILE:n_loc] = kvn_ref[0]
    kv_scr[n_loc:n_key] = kvc_ref[0]
    t0 = i * ATTN_TILE
    row = lax.broadcasted_iota(jnp.int32, (ATTN_TILE, n_loc), 0)
    col = lax.broadcasted_iota(jnp.int32, (ATTN_TILE, n_loc), 1)
    lo = jnp.maximum(row, WINDOW - t0)
    hi = jnp.minimum(row + 2 * WINDOW, n_tok - t0 + WINDOW - 1)
    valid = jnp.logical_and(col >= lo, col <= hi)
    _attn_heads(sink_ref, q_ref, kv_scr, o_ref, jnp.where(valid, 0.0, NEG_BIG), n_key)


def _attn_context_kernel(sink_ref, q_ref, kvc_ref, o_ref, kv_scr, *, n_ctx):
    kv_scr[...] = kvc_ref[0]
    _attn_heads(sink_ref, q_ref, kv_scr, o_ref, None, n_ctx)


def _attn_calls(sink, qa, kv, n_tok, n_ctx, with_ctx):
    b = qa.shape[0]
    kvw = kv.shape[2]
    per = ATTN_TILE // WINDOW
    ctx_blk = n_tok // n_ctx
    last_half = (n_tok + n_ctx) // WINDOW - 1
    smem = pl.BlockSpec(memory_space=pltpu.SMEM)
    ya = pl.pallas_call(
        functools.partial(_attn_latent_kernel, n_tok=n_tok, n_ctx=n_ctx),
        grid=(b, n_tok // ATTN_TILE),
        in_specs=[
            smem,
            pl.BlockSpec((1, ATTN_TILE, W_ATTN), lambda bi, i: (bi, i, 0)),
            pl.BlockSpec((1, WINDOW, kvw), lambda bi, i: (bi, jnp.maximum(per * i - 1, 0), 0)),
            pl.BlockSpec((1, ATTN_TILE, kvw), lambda bi, i: (bi, i, 0)),
            pl.BlockSpec((1, WINDOW, kvw), lambda bi, i: (bi, jnp.minimum(per * i + per, last_half), 0)),
            pl.BlockSpec((1, n_ctx, kvw), lambda bi, i: (bi, ctx_blk, 0)),
        ],
        out_specs=pl.BlockSpec((1, ATTN_TILE, W_ATTN), lambda bi, i: (bi, i, 0)),
        out_shape=jax.ShapeDtypeStruct((b, n_tok, W_ATTN), BF16),
        scratch_shapes=[pltpu.VMEM((ATTN_TILE + 2 * WINDOW + n_ctx, kvw), BF16)],
        compiler_params=_params(("parallel", "parallel"), V7X_VMEM_LIMIT_BYTES),
        name="attention",
    )(sink, qa, kv, kv, kv, kv)
    if not with_ctx:
        return ya, None
    ya_ctx = pl.pallas_call(
        functools.partial(_attn_context_kernel, n_ctx=n_ctx),
        grid=(b,),
        in_specs=[
            smem,
            pl.BlockSpec((1, n_ctx, W_ATTN), lambda bi: (bi, ctx_blk, 0)),
            pl.BlockSpec((1, n_ctx, kvw), lambda bi: (bi, ctx_blk, 0)),
        ],
        out_specs=pl.BlockSpec((1, n_ctx, W_ATTN), lambda bi: (bi, 0, 0)),
        out_shape=jax.ShapeDtypeStruct((b, n_ctx, W_ATTN), BF16),
        scratch_shapes=[pltpu.VMEM((n_ctx, kvw), BF16)],
        compiler_params=_params(("parallel",)),
        name="attention_ctx",
    )(sink, qa, kv)
    return ya, ya_ctx


def _scan_tile(step, n_lat, n_ctx_tiles, reverse):
    if reverse:
        return jnp.where(step < n_ctx_tiles, n_lat + n_ctx_tiles - 1 - step, n_lat - 1 - (step - n_ctx_tiles))
    return jnp.where(step < n_ctx_tiles, n_lat + step, step - n_ctx_tiles)


def _s5_scan_tile(u_ref, bblk_ref, cblk_ref, lam_ref, x_scr, st_scr, *, reverse, n_batch):
    i = pl.program_id(0)

    @pl.when(i == 0)
    def _():
        st_scr[...] = jnp.zeros_like(st_scr)

    u = u_ref[...]
    x_scr[...] = _dot(u.astype(BF16), bblk_ref[...])
    lre = jnp.broadcast_to(lam_ref[0:1, :], (n_batch, S5_MODES))
    lim = jnp.broadcast_to(lam_ref[1:2, :], (n_batch, S5_MODES))

    def step(j, carry):
        re, im = carry
        t = (S5_TILE - 1 - j) if reverse else j
        r0 = pl.multiple_of(t * n_batch, n_batch)
        bre = x_scr[pl.ds(r0, n_batch), 0:S5_MODES]
        bim = x_scr[pl.ds(r0, n_batch), S5_MODES:2 * S5_MODES]
        nre = lre * re - lim * im + bre
        nim = lre * im + lim * re + bim
        x_scr[pl.ds(r0, n_batch), 0:S5_MODES] = nre
        x_scr[pl.ds(r0, n_batch), S5_MODES:2 * S5_MODES] = nim
        return nre, nim

    re, im = lax.fori_loop(0, S5_TILE, step, (st_scr[0], st_scr[1]), unroll=8)
    st_scr[0] = re
    st_scr[1] = im
    return u, _dot(x_scr[...].astype(BF16), cblk_ref[...])


def _s5_fwd_kernel(u_ref, bblk_ref, cblk_ref, lam_ref, o_ref, x_scr, st_scr, *, n_batch):
    _, y = _s5_scan_tile(u_ref, bblk_ref, cblk_ref, lam_ref, x_scr, st_scr, reverse=False, n_batch=n_batch)
    o_ref[...] = y


def _s5_bwd_kernel(u_ref, bblk_ref, cblk_ref, lam_ref, yf_ref, dsk_ref, wg_ref, bg_ref, o_ref, x_scr, st_scr,
                   *, n_batch):
    u, y = _s5_scan_tile(u_ref, bblk_ref, cblk_ref, lam_ref, x_scr, st_scr, reverse=True, n_batch=n_batch)
    y = _gelu_tanh(y + yf_ref[...] + dsk_ref[...] * u)
    o_ref[...] = (y * _sigmoid(_dot(y.astype(BF16), wg_ref[...]) + bg_ref[...])).astype(BF16)


def _s5_calls(u_rows, prm, n_batch, n_tok, n_ctx):
    rows_total = u_rows.shape[0]
    tile_rows = S5_TILE * n_batch
    nt = rows_total // tile_rows
    n_lat, nct = n_tok // S5_TILE, n_ctx // S5_TILE

    def fwd_tile(i):
        return (_scan_tile(i, n_lat, nct, False), 0)

    def bwd_tile(i):
        return (_scan_tile(i, n_lat, nct, True), 0)

    scratch = [pltpu.VMEM((tile_rows, 2 * S5_MODES), F32), pltpu.VMEM((2, n_batch, S5_MODES), F32)]
    wspecs = [_resident((W_S5, 2 * S5_MODES)), _resident((2 * S5_MODES, W_S5)), _resident((2, S5_MODES))]
    yf = pl.pallas_call(
        functools.partial(_s5_fwd_kernel, n_batch=n_batch),
        grid=(nt,),
        in_specs=[pl.BlockSpec((tile_rows, W_S5), fwd_tile)] + wspecs,
        out_specs=pl.BlockSpec((tile_rows, W_S5), fwd_tile),
        out_shape=jax.ShapeDtypeStruct((rows_total, W_S5), F32),
        scratch_shapes=scratch,
        compiler_params=_params(("arbitrary",), V7X_VMEM_LIMIT_BYTES),
        name="s5_forward",
    )(u_rows, prm["bblk"][0], prm["cblk"], prm["lam"][0])
    return pl.pallas_call(
        functools.partial(_s5_bwd_kernel, n_batch=n_batch),
        grid=(nt,),
        in_specs=[pl.BlockSpec((tile_rows, W_S5), bwd_tile)] + wspecs + [
            pl.BlockSpec((tile_rows, W_S5), bwd_tile),
            _resident((1, W_S5)),
            _resident((W_S5, W_S5)),
            _resident((1, W_S5)),
        ],
        out_specs=pl.BlockSpec((tile_rows, W_S5), bwd_tile),
        out_shape=jax.ShapeDtypeStruct((rows_total, W_S5), BF16),
        scratch_shapes=scratch,
        compiler_params=_params(("arbitrary",), V7X_VMEM_LIMIT_BYTES),
        name="s5_backward",
    )(u_rows, prm["bblk"][1], prm["cblk"], prm["lam"][1], yf, prm["dsk"], prm["wglu"], prm["bglu"])


def _s5_prep(a_re, a_im, log_dt, b_re, b_im, c_re, c_im, d_skip, glu_w, glu_b):
    eye = jnp.eye(S5_GROUPS, dtype=F32)
    bblk, lam = [], []
    for d in range(2):
        dt = jnp.exp(log_dt[d])[:, None]
        dre, dim = dt * a_re[d], dt * a_im[d]
        mag = jnp.exp(dre)
        lam_re, lam_im = mag * jnp.cos(dim), mag * jnp.sin(dim)
        den = a_re[d] * a_re[d] + a_im[d] * a_im[d]
        coef_re = ((lam_re - 1.0) * a_re[d] + lam_im * a_im[d]) / den
        coef_im = (lam_im * a_re[d] - (lam_re - 1.0) * a_im[d]) / den
        cr, ci = coef_re[..., None], coef_im[..., None]
        bb_re = cr * b_re - ci * b_im
        bb_im = cr * b_im + ci * b_re
        blk = [jnp.einsum("gpc,gh->gchp", bb, eye).reshape(W_S5, S5_MODES) for bb in (bb_re, bb_im)]
        bblk.append(jnp.concatenate(blk, axis=1).astype(BF16))
        lam.append(jnp.stack([lam_re.reshape(-1), lam_im.reshape(-1)]))
    cblk = [jnp.einsum("gcp,gh->gphc", cc, eye).reshape(S5_MODES, W_S5) for cc in (c_re, -c_im)]
    return dict(bblk=bblk, lam=lam, cblk=jnp.concatenate(cblk, axis=0).astype(BF16),
                dsk=d_skip.reshape(1, W_S5), wglu=glu_w.astype(BF16), bglu=glu_b.reshape(1, W_S5))


def _head_blocks(cols):
    shape = (cols[0].shape[0], LANES)
    first = lax.broadcasted_iota(jnp.int32, shape, 1) < ML_HEAD_DIM
    return jnp.concatenate([jnp.where(first, cols[0], cols[1]), jnp.where(first, cols[2], cols[3])], axis=1)


def _mlstm_chunk(bb, qkv_ref, gm_ref, gmt_ref, brow_ref, bcol_ref, ct_scr, n_scr, mc_scr, mr_scr, *, reverse):
    n = ML_CHUNK
    g0 = 2 * ML_HEADS * (1 if reverse else 0)
    last = 0 if reverse else n - 1
    q = qkv_ref[bb, :, 0:W_ML]
    k = qkv_ref[bb, :, W_ML:2 * W_ML]
    v = qkv_ref[bb, :, 2 * W_ML:3 * W_ML]
    ti = lax.broadcasted_iota(jnp.int32, (n, n), 0)
    si = lax.broadcasted_iota(jnp.int32, (n, n), 1)
    causal = (si >= ti) if reverse else (si <= ti)
    tri = jnp.where(causal, 1.0, 0.0)
    tri_t = jnp.where((ti >= si) if reverse else (ti <= si), 1.0, 0.0)

    g_col = gm_ref[bb] + brow_ref[...]
    lf_col = _log_sigmoid(pltpu.roll(g_col, LANES - ML_HEADS, 1))
    bcum = _dot_exact(tri, lf_col)
    r_col = g_col - bcum
    rowi = lax.broadcasted_iota(jnp.int32, (n, LANES), 0)
    cmax = r_col
    sh = 1
    while sh < n:
        if reverse:
            cmax = jnp.where(rowi < n - sh, jnp.maximum(cmax, pltpu.roll(cmax, n - sh, 0)), cmax)
        else:
            cmax = jnp.where(rowi >= sh, jnp.maximum(cmax, pltpu.roll(cmax, sh, 0)), cmax)
        sh *= 2
    m_col = mc_scr[bb, 0:1, :]
    mm = jnp.maximum(m_col, cmax)
    w_inter = jnp.exp(m_col - mm)
    floor = jnp.exp(-(bcum + mm))
    mm_last = mm[last:last + 1, :]
    w_state = jnp.exp(r_col - mm_last)
    mc_scr[bb, 0:1, :] = bcum[last:last + 1, :] + mm_last

    g_row = gmt_ref[bb, g0:g0 + 2 * ML_HEADS, :] + bcol_ref[g0:g0 + 2 * ML_HEADS, :]
    lf_row = _log_sigmoid(g_row)
    bcum_row = _dot_exact(lf_row, tri_t)
    r_row = g_row[0:ML_HEADS] - bcum_row[ML_HEADS:2 * ML_HEADS]
    m_row = mr_scr[bb, 0:ML_HEADS, :]
    mm_row = jnp.maximum(m_row, jnp.max(r_row, axis=1, keepdims=True))
    decay_row = jnp.exp(m_row - mm_row)
    mr_scr[bb, 0:ML_HEADS, :] = jnp.sum(lf_row[ML_HEADS:2 * ML_HEADS], axis=1, keepdims=True) + mm_row
    decay_col = jnp.concatenate(
        [jnp.broadcast_to(decay_row[h:h + 1, :], (ML_HEAD_DIM, LANES)) for h in range(ML_HEADS)], axis=0)

    lane_head = lax.broadcasted_iota(jnp.int32, (n, W_ML), 1) // ML_HEAD_DIM
    zero_b = jnp.zeros((n, W_ML), BF16)
    kexp = jnp.concatenate([jnp.where(lane_head == h, k, zero_b) for h in range(ML_HEADS)], axis=0)
    vexp = jnp.concatenate([jnp.where(lane_head == h, v, zero_b) for h in range(ML_HEADS)], axis=0)
    s_all = _dot_nt(q, kexp)
    w_list, rowsum = [], []
    for h in range(ML_HEADS):
        mm_b = jnp.broadcast_to(mm[:, g0 + h:g0 + h + 1], (n, n))
        arg = jnp.where(causal, r_row[h:h + 1, :] - mm_b, NEG_BIG)
        w = jnp.exp(arg) * s_all[:, h * n:(h + 1) * n]
        rowsum.append(jnp.sum(w, axis=1, keepdims=True))
        w_list.append(w.astype(BF16))
    num_intra = _dot(jnp.concatenate(w_list, axis=1), vexp)

    ct = ct_scr[bb]
    nst = n_scr[bb]
    bi = lax.broadcasted_iota(jnp.int32, (W_ML, W_ML), 0) // ML_HEAD_DIM
    bj = lax.broadcasted_iota(jnp.int32, (W_ML, W_ML), 1) // ML_HEAD_DIM
    same_head = bi == bj
    nblk = jnp.where(same_head, jnp.concatenate([nst, nst], axis=1), 0.0)
    q_c = _dot(q, ct.astype(BF16))
    q_n = _dot(q, nblk.astype(BF16))

    def per_head(x):
        return _head_blocks([x[:, g0 + h:g0 + h + 1] for h in range(ML_HEADS)])

    wi_b = per_head(w_inter)
    num = wi_b * q_c + num_intra
    den = wi_b * q_n + _head_blocks(rowsum)
    h_out = num / jnp.maximum(jnp.abs(den), per_head(floor))

    kw = (k.astype(F32) * per_head(w_state)).astype(BF16)
    vext = jnp.concatenate([v, jnp.ones((n, LANES), BF16)], axis=1)
    upd = _dot_tn(kw, vext)
    ct_scr[bb] = jnp.concatenate([decay_col, decay_col], axis=1) * ct + jnp.where(same_head, upd[:, 0:W_ML], 0.0)
    n_scr[bb] = decay_col * nst + upd[:, W_ML:W_ML + LANES]
    return h_out


def _mlstm_reset(ct_scr, n_scr, mc_scr, mr_scr):
    @pl.when(pl.program_id(1) == 0)
    def _():
        ct_scr[...] = jnp.zeros_like(ct_scr)
        n_scr[...] = jnp.zeros_like(n_scr)
        mc_scr[...] = jnp.zeros_like(mc_scr)
        mr_scr[...] = jnp.zeros_like(mr_scr)


def _mlstm_fwd_kernel(qkv_ref, gm_ref, gmt_ref, brow_ref, bcol_ref, hf_ref, *state):
    _mlstm_reset(*state)
    for bb in range(ML_BATCH):
        hf_ref[bb] = _mlstm_chunk(bb, qkv_ref, gm_ref, gmt_ref, brow_ref, bcol_ref, *state, reverse=False)


def _mlstm_bwd_kernel(qkv_ref, gm_ref, gmt_ref, brow_ref, bcol_ref, hf_ref, om_ref, ng_ref, o_ref, *state):
    _mlstm_reset(*state)
    lane_head = lax.broadcasted_iota(jnp.int32, (ML_CHUNK, W_ML), 1) // ML_HEAD_DIM
    for bb in range(ML_BATCH):
        ht = _mlstm_chunk(bb, qkv_ref, gm_ref, gmt_ref, brow_ref, bcol_ref, *state, reverse=True) + hf_ref[bb]
        sq = ht * ht
        ms = [jnp.sum(jnp.where(lane_head == h, sq, 0.0), axis=1, keepdims=True) * (1.0 / ML_HEAD_DIM)
              for h in range(ML_HEADS)]
        hn = ht * lax.rsqrt(_head_blocks(ms) + NORM_EPS) * ng_ref[...]
        o_ref[bb] = (hn * _sigmoid(om_ref[bb])).astype(BF16)


def _mlstm_calls(qkvm, gm, gmt, om, bias_row, bias_col, norm_g, n_tok, n_ctx):
    b, tc, _ = qkvm.shape
    nt = tc // ML_CHUNK
    n_lat, nct = n_tok // ML_CHUNK, n_ctx // ML_CHUNK
    nb = ML_BATCH
    state = [
        pltpu.VMEM((nb, W_ML, W_ML), F32),
        pltpu.VMEM((nb, W_ML, LANES), F32),
        pltpu.VMEM((nb, 8, LANES), F32),
        pltpu.VMEM((nb, 8, LANES), F32),
    ]

    def specs(reverse):
        def tile(g, s):
            return (g, _scan_tile(s, n_lat, nct, reverse), 0)

        def tile_t(g, s):
            return (g, 0, _scan_tile(s, n_lat, nct, reverse))

        row = lambda w: pl.BlockSpec((nb, ML_CHUNK, w), tile)
        common = [row(3 * W_ML), row(LANES), pl.BlockSpec((nb, N_GATES, ML_CHUNK), tile_t),
                  _resident((1, LANES)), _resident((N_GATES, LANES))]
        return common, row

    common, row = specs(False)
    hf = pl.pallas_call(
        _mlstm_fwd_kernel,
        grid=(b // nb, nt),
        in_specs=common,
        out_specs=row(W_ML),
        out_shape=jax.ShapeDtypeStruct((b, tc, W_ML), F32),
        scratch_shapes=state,
        compiler_params=_params(("parallel", "arbitrary"), V7X_VMEM_LIMIT_BYTES),
        name="mlstm_forward",
    )(qkvm, gm, gmt, bias_row, bias_col)
    common, row = specs(True)
    return pl.pallas_call(
        _mlstm_bwd_kernel,
        grid=(b // nb, nt),
        in_specs=common + [row(W_ML), row(W_ML), _resident((1, W_ML))],
        out_specs=row(W_ML),
        out_shape=jax.ShapeDtypeStruct((b, tc, W_ML), BF16),
        scratch_shapes=state,
        compiler_params=_params(("parallel", "arbitrary"), V7X_VMEM_LIMIT_BYTES),
        name="mlstm_backward",
    )(qkvm, gm, gmt, bias_row, bias_col, hf, om, norm_g.reshape(1, W_ML))


def _merge_kernel(ya_ref, ys_ref, ym_ref, gb_ref, x_ref, mod_ref, g_ref, wa_ref, ws_ref, wm_ref, wo_ref,
                  x1_ref, h2_ref, *, d_model):
    d = d_model
    mod = mod_ref[0]
    y = (_sigmoid(gb_ref[0, :, 0:d].astype(F32)) * _dot(ya_ref[0], wa_ref[...])
         + _sigmoid(gb_ref[0, :, d:2 * d].astype(F32)) * _dot(ys_ref[0], ws_ref[...])
         + _sigmoid(gb_ref[0, :, 2 * d:3 * d].astype(F32)) * _dot(ym_ref[0], wm_ref[...]))
    x1 = x_ref[0] + mod[:, 2 * d:3 * d] * _dot(y.astype(BF16), wo_ref[...])
    x1_ref[0] = x1
    h2_ref[0] = (_rms(x1) * g_ref[...] * (1.0 + mod[:, 4 * d:5 * d]) + mod[:, 3 * d:4 * d]).astype(BF16)


def _merge_call(ya, ys, ym, gb, x, modsel, mod_off, norm_g, wa, ws, wm, wo, tile, blk_off):
    b, n, d = x.shape

    def own(bi, i):
        return (bi, i, 0)

    def shared(bi, i):
        return (bi, i + blk_off, 0)

    return pl.pallas_call(
        functools.partial(_merge_kernel, d_model=d),
        grid=(b, n // tile),
        in_specs=[
            pl.BlockSpec((1, tile, W_ATTN), own),
            pl.BlockSpec((1, tile, W_S5), shared),
            pl.BlockSpec((1, tile, W_ML), shared),
            pl.BlockSpec((1, tile, N_BRANCH * d), shared),
            pl.BlockSpec((1, tile, d), own),
            pl.BlockSpec((1, 1, 6 * d), lambda bi, i: (mod_off + bi, 0, 0)),
            _resident((1, d)),
            _resident((W_ATTN, d)),
            _resident((W_S5, d)),
            _resident((W_ML, d)),
            _resident((d, d)),
        ],
        out_specs=[pl.BlockSpec((1, tile, d), own), pl.BlockSpec((1, tile, d), own)],
        out_shape=(jax.ShapeDtypeStruct((b, n, d), F32), jax.ShapeDtypeStruct((b, n, d), BF16)),
        compiler_params=_params(("parallel", "parallel"), V7X_VMEM_LIMIT_BYTES),
        name="merge",
    )(ya, ys, ym, gb, x, modsel, norm_g.reshape(1, d), wa, ws, wm, wo)


def _ffn_kernel(hp_ref, hm_ref, hn_ref, x1_ref, mod_ref, wup_ref, cw_ref, cb_ref, wdn_ref, fg_ref, o_ref,
                lhs_scr, *, d_model, d_ff, tile, final):
    d = d_model
    halo = BF16_SUBLANES
    rows = tile + 2 * halo
    i = pl.program_id(1)
    prev_ok = i >= 1
    next_ok = i < pl.num_programs(1) - 1
    lhs_scr[0:halo] = jnp.where(prev_ok, hp_ref[0], jnp.zeros_like(hp_ref[0]))
    lhs_scr[halo:halo + tile] = hm_ref[0]
    lhs_scr[halo + tile:rows] = jnp.where(next_ok, hn_ref[0], jnp.zeros_like(hn_ref[0]))
    lhs = lhs_scr[...]
    acc = jnp.zeros((tile, d), F32)
    for j in range(d_ff // FFN_CHUNK):
        parts = []
        for lo in (j * FFN_CHUNK, d_ff + j * FFN_CHUNK):
            u = _dot(lhs, wup_ref[:, lo:lo + FFN_CHUNK])
            cw = cw_ref[:, lo:lo + FFN_CHUNK]
            parts.append(pltpu.roll(u, 1, 0)[halo:halo + tile] * cw[0:1]
                         + u[halo:halo + tile] * cw[1:2]
                         + pltpu.roll(u, rows - 1, 0)[halo:halo + tile] * cw[2:3]
                         + cb_ref[:, lo:lo + FFN_CHUNK])
        a, v = parts
        act = (a * _sigmoid(a) * v).astype(BF16)
        acc = acc + _dot(act, wdn_ref[j * FFN_CHUNK:(j + 1) * FFN_CHUNK, :])
    out = x1_ref[0] + mod_ref[0][:, 5 * d:6 * d] * acc
    if final:
        out = _rms(out) * fg_ref[...]
    o_ref[0] = out


def _ffn_call(h2, x1, modsel, mod_off, w_up, conv_w, conv_b, w_down, final_g, tile, *, final):
    b, n, d = x1.shape
    d_ff = w_down.shape[0]
    halo = BF16_SUBLANES
    per = tile // halo
    n_halo = n // halo
    return pl.pallas_call(
        functools.partial(_ffn_kernel, d_model=d, d_ff=d_ff, tile=tile, final=final),
        grid=(b, n // tile),
        in_specs=[
            pl.BlockSpec((1, halo, d), lambda bi, i: (bi, jnp.maximum(i * per - 1, 0), 0)),
            pl.BlockSpec((1, tile, d), lambda bi, i: (bi, i, 0)),
            pl.BlockSpec((1, halo, d), lambda bi, i: (bi, jnp.minimum((i + 1) * per, n_halo - 1), 0)),
            pl.BlockSpec((1, tile, d), lambda bi, i: (bi, i, 0)),
            pl.BlockSpec((1, 1, 6 * d), lambda bi, i: (mod_off + bi, 0, 0)),
            _resident((d, 2 * d_ff)),
            _resident((3, 2 * d_ff)),
            _resident((1, 2 * d_ff)),
            _resident((d_ff, d)),
            _resident((1, d)),
        ],
        out_specs=pl.BlockSpec((1, tile, d), lambda bi, i: (bi, i, 0)),
        out_shape=jax.ShapeDtypeStruct((b, n, d), F32),
        scratch_shapes=[pltpu.VMEM((tile + 2 * halo, d), BF16)],
        compiler_params=_params(("parallel", "parallel"), V7X_VMEM_LIMIT_BYTES),
        name="conv_ffn",
    )(h2, h2, h2, x1, modsel, w_up, conv_w, conv_b.reshape(1, 2 * d_ff), w_down, final_g.reshape(1, d))


def _prep_w_in(w):
    d = w.shape[0]
    k0, k1 = w[:, W_ATTN:W_ATTN + HEAD_DIM], w[:, W_ATTN + HEAD_DIM:W_ATTN + 2 * HEAD_DIM]
    o = W_ATTN + 2 * HEAD_DIM
    v0, v1 = w[:, o:o + HEAD_DIM], w[:, o + HEAD_DIM:o + 2 * HEAD_DIM]
    o += 2 * HEAD_DIM
    rest = w[:, o:o + W_S5 + 4 * W_ML]
    o += W_S5 + 4 * W_ML
    gm = w[:, o:o + N_GATES]
    gb = w[:, o + N_GATES:o + N_GATES + N_BRANCH * d]
    cols = [w[:, 0:W_ATTN], k0, k0, k1, k1, v0, v0, v1, v1, rest, gb, jnp.tile(gm, (1, LANES // N_GATES))]
    return jnp.concatenate(cols, axis=1).astype(BF16), gm.T.astype(BF16)


def _rope_tables(n_tok):
    pos = jnp.arange(n_tok)
    row = (pos // GRID_W).astype(F32)
    col = (pos % GRID_W).astype(F32)
    n_freq = HEAD_DIM // 4
    inv_freq = ROPE_BASE ** (-jnp.arange(n_freq, dtype=F32) / n_freq)
    lane = jnp.arange(LANES)
    dd = lane % HEAD_DIM
    by_col = (dd // (HEAD_DIM // 2)) == 1
    d32 = dd % (HEAD_DIM // 2)
    first = d32 < n_freq
    freq = inv_freq[d32 % n_freq]
    ang = jnp.where(by_col[None, :], col[:, None], row[:, None]) * freq[None, :]
    cos, sin = jnp.cos(ang), jnp.sin(ang)
    return cos, jnp.where(first[None, :], -sin, 0.0), jnp.where(first[None, :], 0.0, sin)


def kernel(x, c, ctx, c_ctx, mod_w, mod_b, norm1_g, norm2_g, w_in, attn_sink, s5_a_re, s5_a_im, s5_log_dt, s5_b_re, s5_b_im, s5_c_re, s5_c_im, s5_d, s5_glu_w, s5_glu_b, ml_igate_b, ml_fgate_b, ml_norm_g, w_branch_attn, w_branch_s5, w_branch_ml, w_out, ffn_w_up, ffn_conv_w, ffn_conv_b, ffn_w_down, final_norm_g):
    b, n_tok, d = x.shape
    n_ctx = ctx.shape[1]
    n_layer = mod_w.shape[0]
    assert n_tok % ROW_TILE == 0 and n_tok % FFN_TILE == 0 and n_tok % n_ctx == 0 and n_tok % GRID_W == 0
    assert n_ctx % ATTN_TILE == 0 and n_ctx % ML_CHUNK == 0 and n_ctx % S5_TILE == 0
    assert b % 8 == 0 and b % ML_BATCH == 0 and b + 1 <= 16 and d % LANES == 0
    tc = n_tok + n_ctx

    c_all = jnp.zeros((16, d), F32).at[:b].set(c).at[b].set(c_ctx)
    mod = _mod_call(c_all, mod_w.astype(BF16), mod_b)
    rope = _rope_tables(n_tok)
    xl, xk = x, ctx

    for l in range(n_layer):
        final = l == n_layer - 1
        modsel = jnp.concatenate([jnp.broadcast_to(mod[l, b:b + 1], (b, 6 * d)), mod[l, :b]]).reshape(2 * b, 1, 6 * d)
        w_in_l, w_gt = _prep_w_in(w_in[l])
        outs = _in_call(xl, modsel, b, norm1_g[l], w_in_l, w_gt, tc, ROW_TILE, 0, rope=rope)
        outs = _in_call(xk, modsel, 0, norm1_g[l], w_in_l, w_gt, tc, n_ctx, n_tok // n_ctx, prev=outs)
        qa, kv, us, qkvm, om, gb, gm, gmt = outs

        ya, ya_ctx = _attn_calls(attn_sink[l], qa, kv, n_tok, n_ctx, with_ctx=not final)

        s5p = _s5_prep(s5_a_re[l], s5_a_im[l], s5_log_dt[l], s5_b_re[l], s5_b_im[l], s5_c_re[l], s5_c_im[l],
                       s5_d[l], s5_glu_w[l], s5_glu_b[l])
        u_rows = us.transpose(1, 0, 2).reshape(tc * b, W_S5)
        ys = _s5_calls(u_rows, s5p, b, n_tok, n_ctx).reshape(tc, b, W_S5).transpose(1, 0, 2)

        gate_b = jnp.concatenate([ml_igate_b[l, 0], ml_fgate_b[l, 0], ml_igate_b[l, 1], ml_fgate_b[l, 1]])
        bias_row = jnp.tile(gate_b, LANES // N_GATES).reshape(1, LANES)
        bias_col = jnp.broadcast_to(gate_b[:, None], (N_GATES, LANES))
        ym = _mlstm_calls(qkvm, gm, gmt, om, bias_row, bias_col, ml_norm_g[l], n_tok, n_ctx)

        wb = [w.astype(BF16) for w in (w_branch_attn[l], w_branch_s5[l], w_branch_ml[l], w_out[l])]
        wf = (ffn_w_up[l].astype(BF16), ffn_conv_w[l], ffn_conv_b[l], ffn_w_down[l].astype(BF16), final_norm_g)
        x1, h2 = _merge_call(ya, ys, ym, gb, xl, modsel, b, norm2_g[l], *wb, ROW_TILE, 0)
        xl = _ffn_call(h2, x1, modsel, b, *wf, FFN_TILE, final=final)
        if not final:
            x1, h2 = _merge_call(ya_ctx, ys, ym, gb, xk, modsel, 0, norm2_g[l], *wb, n_ctx, n_tok // n_ctx)
            xk = _ffn_call(h2, x1, modsel, 0, *wf, n_ctx, final=False)
    return xl
```

```python
import functools
import math

import jax
import jax.numpy as jnp
from jax import lax
from jax.experimental import pallas as pl
from jax.experimental.pallas import tpu as pltpu

F32 = jnp.float32
BF16 = jnp.bfloat16

N_HEADS = 8
N_KV_HEADS = 2
HEAD_DIM = 64
Q_PER_KV = N_HEADS // N_KV_HEADS
W_ATTN = N_HEADS * HEAD_DIM
WINDOW = 128
GRID_W = 64
ROPE_BASE = 10000.0
W_S5 = 256
S5_GROUP = 16
S5_GROUPS = W_S5 // S5_GROUP
S5_STATE = 64
S5_MODES = S5_GROUPS * S5_STATE
ML_HEADS = 4
ML_HEAD_DIM = 64
W_ML = ML_HEADS * ML_HEAD_DIM
N_GATES = 4 * ML_HEADS
N_BRANCH = 3
NORM_EPS = 1e-6

LANES = 128
BF16_SUBLANES = 16
V7X_VMEM_LIMIT_BYTES = 56 * 1024 * 1024

ROW_TILE = 512
FFN_TILE = 1024
ATTN_TILE = 256
ML_CHUNK = 256
ML_BATCH = 4
S5_TILE = 128
FFN_CHUNK = 256
NEG_BIG = -1e30
LOG2E = math.log2(math.e)
Q_SCALE = LOG2E * HEAD_DIM ** -0.5

Z_QA = 0
Z_KV = Z_QA + W_ATTN
Z_US = Z_KV + 4 * LANES
Z_QKVM = Z_US + W_S5
Z_OM = Z_QKVM + 3 * W_ML
Z_GB = Z_OM + W_ML


def _sigmoid(x):
    return 1.0 / (1.0 + jnp.exp(-x))


def _log_sigmoid(x):
    return jnp.minimum(x, 0.0) - jnp.log(1.0 + jnp.exp(-jnp.abs(x)))


def _gelu_tanh(x):
    return 0.5 * x * (1.0 + jnp.tanh(math.sqrt(2.0 / math.pi) * (x + 0.044715 * (x * x * x))))


def _rms(x):
    return x * lax.rsqrt(jnp.mean(x * x, axis=-1, keepdims=True) + NORM_EPS)


def _dot(a, b):
    return jnp.dot(a, b, preferred_element_type=F32)


def _dot_nt(a, b):
    return lax.dot_general(a, b, (((1,), (1,)), ((), ())), preferred_element_type=F32)


def _params(sem, vmem=None):
    return pltpu.CompilerParams(dimension_semantics=sem, vmem_limit_bytes=vmem)


def _resident(shape):
    return pl.BlockSpec(shape, lambda *_: (0,) * len(shape), pipeline_mode=pl.Buffered(1))


def _mod_kernel(c_ref, w_ref, b_ref, o_ref):
    c = c_ref[...]
    s = (c * _sigmoid(c)).astype(BF16)
    o_ref[0] = _dot(s, w_ref[0]) + b_ref[0]


def _mod_call(c_all, mod_w, mod_b):
    n_layer, d, d6 = mod_w.shape
    tn = d6 // 4
    rows = c_all.shape[0]
    return pl.pallas_call(
        _mod_kernel,
        grid=(n_layer, d6 // tn),
        in_specs=[
            pl.BlockSpec((rows, d), lambda l, j: (0, 0)),
            pl.BlockSpec((1, d, tn), lambda l, j: (l, 0, j)),
            pl.BlockSpec((1, 1, tn), lambda l, j: (l, 0, j)),
        ],
        out_specs=pl.BlockSpec((1, rows, tn), lambda l, j: (l, 0, j)),
        out_shape=jax.ShapeDtypeStruct((n_layer, rows, d6), F32),
        compiler_params=_params(("parallel", "parallel")),
        name="modulation",
    )(c_all, mod_w, mod_b.reshape(n_layer, 1, d6))


def _rope(xb, cos, s1, s2):
    return xb * cos + pltpu.roll(xb, LANES - 16, 1) * s1 + pltpu.roll(xb, 16, 1) * s2


N_IN_OUT = 9


def _in_kernel(x_ref, mod_ref, g_ref, w_ref, wt_ref, *rest, d_model, rotary):
    d = d_model
    if rotary:
        cos, s1, s2 = (r[...] for r in rest[:3])
        rest = rest[3:]
    qa_ref, kv_ref, us_ref, qkvm_ref, om_ref, gb_ref, gm_ref, gmt_ref, kmt_ref = rest[-N_IN_OUT:]
    k_scale = ML_HEAD_DIM ** -0.5
    mod = mod_ref[0]
    h = (_rms(x_ref[0]) * g_ref[...] * (1.0 + mod[:, d:2 * d]) + mod[:, 0:d]).astype(BF16)

    def proj(lo, width):
        return _dot(h, w_ref[:, lo:lo + width])

    def rot(blk):
        return _rope(blk, cos, s1, s2) if rotary else blk

    zq = proj(Z_QA, W_ATTN)
    for p in range(W_ATTN // LANES):
        sl = slice(p * LANES, (p + 1) * LANES)
        qa_ref[0, :, sl] = (rot(zq[:, sl]) * Q_SCALE).astype(BF16)
    zkv = proj(Z_KV, 4 * LANES)
    for p in range(4):
        sl = slice(p * LANES, (p + 1) * LANES)
        kv_ref[0, :, sl] = (rot(zkv[:, sl]) if p < 2 else zkv[:, sl]).astype(BF16)
    us_ref[0] = proj(Z_US, W_S5)
    zm = proj(Z_QKVM, 3 * W_ML)
    qkvm_ref[0, :, 0:W_ML] = zm[:, 0:W_ML].astype(BF16)
    qkvm_ref[0, :, W_ML:2 * W_ML] = (zm[:, W_ML:2 * W_ML] * k_scale).astype(BF16)
    qkvm_ref[0, :, 2 * W_ML:3 * W_ML] = zm[:, 2 * W_ML:3 * W_ML].astype(BF16)
    om_ref[0] = proj(Z_OM, W_ML)
    for p in range(N_BRANCH):
        gb_ref[0, :, p * d:(p + 1) * d] = proj(Z_GB + p * d, d).astype(BF16)
    gm_ref[0] = proj(Z_GB + N_BRANCH * d, LANES)
    zt = _dot_nt(wt_ref[...], h)
    gmt_ref[0] = zt[0:N_GATES]
    kmt_ref[0] = (zt[N_GATES:N_GATES + W_ML] * k_scale).astype(BF16)


def _in_call(x, modsel, mod_off, norm_g, w_in, w_t, tc, tile, blk_off, rope=None, prev=None):
    b, n, d = x.shape
    nz = w_in.shape[1]
    widths = (W_ATTN, 4 * LANES, W_S5, 3 * W_ML, W_ML, N_BRANCH * d, LANES)
    dtypes = (BF16, BF16, F32, BF16, F32, BF16, F32)
    out_shapes = [jax.ShapeDtypeStruct((b, tc, w), t) for w, t in zip(widths, dtypes)]
    out_shapes.append(jax.ShapeDtypeStruct((b, N_GATES, tc), F32))
    out_shapes.append(jax.ShapeDtypeStruct((b, W_ML, tc), BF16))
    out_specs = [pl.BlockSpec((1, tile, w), lambda bi, i: (bi, i + blk_off, 0)) for w in widths]
    out_specs.append(pl.BlockSpec((1, N_GATES, tile), lambda bi, i: (bi, 0, i + blk_off)))
    out_specs.append(pl.BlockSpec((1, W_ML, tile), lambda bi, i: (bi, 0, i + blk_off)))
    in_specs = [
        pl.BlockSpec((1, tile, d), lambda bi, i: (bi, i, 0)),
        pl.BlockSpec((1, 1, 6 * d), lambda bi, i: (mod_off + bi, 0, 0)),
        _resident((1, d)),
        _resident((d, nz)),
        _resident((N_GATES + W_ML, d)),
    ]
    args = [x, modsel, norm_g.reshape(1, d), w_in, w_t]
    if rope is not None:
        in_specs += [pl.BlockSpec((tile, LANES), lambda bi, i: (i, 0))] * 3
        args += list(rope)
    aliases = {}
    if prev is not None:
        aliases = {len(args) + k: k for k in range(N_IN_OUT)}
        in_specs += [pl.BlockSpec(memory_space=pl.ANY)] * N_IN_OUT
        args += list(prev)
    return pl.pallas_call(
        functools.partial(_in_kernel, d_model=d, rotary=rope is not None),
        grid=(b, n // tile),
        in_specs=in_specs,
        out_specs=out_specs,
        out_shape=out_shapes,
        input_output_aliases=aliases,
        compiler_params=_params(("parallel", "parallel"), V7X_VMEM_LIMIT_BYTES),
        name="in_proj",
    )(*args)


def _attn_heads(sink_ref, q_ref, kv_scr, o_ref, bias, n_key):
    lane = lax.broadcasted_iota(jnp.int32, (n_key, LANES), 1)
    n_q = q_ref.shape[1]
    own = lax.broadcasted_iota(jnp.int32, (n_q, LANES), 1) < HEAD_DIM
    n_bias = 0 if bias is None else bias.shape[1]
    for p in range(N_HEADS // 2):
        kvh = p // (Q_PER_KV // 2)
        kd = kv_scr[0:n_key, kvh * LANES:(kvh + 1) * LANES]
        vd = kv_scr[0:n_key, (N_KV_HEADS + kvh) * LANES:(N_KV_HEADS + kvh + 1) * LANES]
        qb = q_ref[0, :, p * LANES:(p + 1) * LANES]
        outs = []
        for par in range(2):
            head = 2 * p + par
            half = (lane >= HEAD_DIM) if par else (lane < HEAD_DIM)
            km = jnp.where(half, kd, jnp.zeros_like(kd))
            vm = jnp.where(half, vd, jnp.ones_like(vd))
            s = _dot_nt(qb, km)
            parts = [s[:, 0:n_bias] + bias, s[:, n_bias:n_key]] if n_bias else [s]
            sk = sink_ref[head] * LOG2E
            m = sk
            for part in parts:
                m = jnp.maximum(m, jnp.max(part, axis=1, keepdims=True))
            e = jnp.concatenate([jnp.exp2(part - m).astype(BF16) for part in parts], axis=1)
            o = _dot(e, vm)
            den = pltpu.roll(o, HEAD_DIM, 1) + jnp.exp2(sk - m)
            outs.append(o / den)
        o_ref[0, :, p * LANES:(p + 1) * LANES] = jnp.where(own, outs[0], outs[1]).astype(BF16)


def _attn_latent_kernel(sink_ref, q_ref, kvp_ref, kvm_ref, kvn_ref, kvc_ref, o_ref, kv_scr, *, n_tok, n_ctx):
    i = pl.program_id(1)
    n_loc = ATTN_TILE + 2 * WINDOW
    n_key = n_loc + n_ctx
    kv_scr[0:WINDOW] = kvp_ref[0]
    kv_scr[WINDOW:WINDOW + ATTN_TILE] = kvm_ref[0]
    kv_scr[WINDOW + ATTN_TILE:n_loc] = kvn_ref[0]
    kv_scr[n_loc:n_key] = kvc_ref[0]
    t0 = i * ATTN_TILE
    row = lax.broadcasted_iota(jnp.int32, (ATTN_TILE, n_loc), 0)
    col = lax.broadcasted_iota(jnp.int32, (ATTN_TILE, n_loc), 1)
    lo = jnp.maximum(row, WINDOW - t0)
    hi = jnp.minimum(row + 2 * WINDOW, n_tok - t0 + WINDOW - 1)
    valid = jnp.logical_and(col >= lo, col <= hi)
    _attn_heads(sink_ref, q_ref, kv_scr, o_ref, jnp.where(valid, 0.0, NEG_BIG), n_key)


def _attn_context_kernel(sink_ref, q_ref, kvc_ref, o_ref, kv_scr, *, n_ctx):
    kv_scr[...] = kvc_ref[0]
    _attn_heads(sink_ref, q_ref, kv_scr, o_ref, None, n_ctx)


def _attn_calls(sink, qa, kv, n_tok, n_ctx, with_ctx):
    b = qa.shape[0]
    kvw = kv.shape[2]
    per = ATTN_TILE // WINDOW
    ctx_blk = n_tok // n_ctx
    last_half = (n_tok + n_ctx) // WINDOW - 1
    smem = pl.BlockSpec(memory_space=pltpu.SMEM)
    ya = pl.pallas_call(
        functools.partial(_attn_latent_kernel, n_tok=n_tok, n_ctx=n_ctx),
        grid=(b, n_tok // ATTN_TILE),
        in_specs=[
            smem,
            pl.BlockSpec((1, ATTN_TILE, W_ATTN), lambda bi, i: (bi, i, 0)),
            pl.BlockSpec((1, WINDOW, kvw), lambda bi, i: (bi, jnp.maximum(per * i - 1, 0), 0)),
            pl.BlockSpec((1, ATTN_TILE, kvw), lambda bi, i: (bi, i, 0)),
            pl.BlockSpec((1, WINDOW, kvw), lambda bi, i: (bi, jnp.minimum(per * i + per, last_half), 0)),
            pl.BlockSpec((1, n_ctx, kvw), lambda bi, i: (bi, ctx_blk, 0)),
        ],
        out_specs=pl.BlockSpec((1, ATTN_TILE, W_ATTN), lambda bi, i: (bi, i, 0)),
        out_shape=jax.ShapeDtypeStruct((b, n_tok, W_ATTN), BF16),
        scratch_shapes=[pltpu.VMEM((ATTN_TILE + 2 * WINDOW + n_ctx, kvw), BF16)],
        compiler_params=_params(("parallel", "parallel"), V7X_VMEM_LIMIT_BYTES),
        name="attention",
    )(sink, qa, kv, kv, kv, kv)
    if not with_ctx:
        return ya, None
    ya_ctx = pl.pallas_call(
        functools.partial(_attn_context_kernel, n_ctx=n_ctx),
        grid=(b,),
        in_specs=[
            smem,
            pl.BlockSpec((1, n_ctx, W_ATTN), lambda bi: (bi, ctx_blk, 0)),
            pl.BlockSpec((1, n_ctx, kvw), lambda bi: (bi, ctx_blk, 0)),
        ],
        out_specs=pl.BlockSpec((1, n_ctx, W_ATTN), lambda bi: (bi, 0, 0)),
        out_shape=jax.ShapeDtypeStruct((b, n_ctx, W_ATTN), BF16),
        scratch_shapes=[pltpu.VMEM((n_ctx, kvw), BF16)],
        compiler_params=_params(("parallel",)),
        name="attention_ctx",
    )(sink, qa, kv)
    return ya, ya_ctx


def _scan_tile(step, n_lat, n_ctx_tiles, reverse):
    if reverse:
        return jnp.where(step < n_ctx_tiles, n_lat + n_ctx_tiles - 1 - step, n_lat - 1 - (step - n_ctx_tiles))
    return jnp.where(step < n_ctx_tiles, n_lat + step, step - n_ctx_tiles)


def _s5_scan_tile(u_ref, bblk_ref, cblk_ref, lam_ref, x_scr, st_scr, *, reverse, n_batch):
    i = pl.program_id(0)

    @pl.when(i == 0)
    def _():
        st_scr[...] = jnp.zeros_like(st_scr)

    u = u_ref[...]
    x_scr[...] = _dot(u.astype(BF16), bblk_ref[...])
    lre = jnp.broadcast_to(lam_ref[0:1, :], (n_batch, S5_MODES))
    lim = jnp.broadcast_to(lam_ref[1:2, :], (n_batch, S5_MODES))

    def step(j, carry):
        re, im = carry
        t = (S5_TILE - 1 - j) if reverse else j
        r0 = pl.multiple_of(t * n_batch, n_batch)
        bre = x_scr[pl.ds(r0, n_batch), 0:S5_MODES]
        bim = x_scr[pl.ds(r0, n_batch), S5_MODES:2 * S5_MODES]
        nre = lre * re - lim * im + bre
        nim = lre * im + lim * re + bim
        x_scr[pl.ds(r0, n_batch), 0:S5_MODES] = nre
        x_scr[pl.ds(r0, n_batch), S5_MODES:2 * S5_MODES] = nim
        return nre, nim

    re, im = lax.fori_loop(0, S5_TILE, step, (st_scr[0], st_scr[1]), unroll=8)
    st_scr[0] = re
    st_scr[1] = im
    return u, _dot(x_scr[...].astype(BF16), cblk_ref[...])


def _s5_fwd_kernel(u_ref, bblk_ref, cblk_ref, lam_ref, o_ref, x_scr, st_scr, *, n_batch):
    _, y = _s5_scan_tile(u_ref, bblk_ref, cblk_ref, lam_ref, x_scr, st_scr, reverse=False, n_batch=n_batch)
    o_ref[...] = y


def _s5_bwd_kernel(u_ref, bblk_ref, cblk_ref, lam_ref, yf_ref, dsk_ref, wg_ref, bg_ref, o_ref, x_scr, st_scr,
                   *, n_batch):
    u, y = _s5_scan_tile(u_ref, bblk_ref, cblk_ref, lam_ref, x_scr, st_scr, reverse=True, n_batch=n_batch)
    y = _gelu_tanh(y + yf_ref[...] + dsk_ref[...] * u)
    o_ref[...] = (y * _sigmoid(_dot(y.astype(BF16), wg_ref[...]) + bg_ref[...])).astype(BF16)


def _s5_calls(u_rows, prm, n_batch, n_tok, n_ctx):
    rows_total = u_rows.shape[0]
    tile_rows = S5_TILE * n_batch
    nt = rows_total // tile_rows
    n_lat, nct = n_tok // S5_TILE, n_ctx // S5_TILE

    def fwd_tile(i):
        return (_scan_tile(i, n_lat, nct, False), 0)

    def bwd_tile(i):
        return (_scan_tile(i, n_lat, nct, True), 0)

    scratch = [pltpu.VMEM((tile_rows, 2 * S5_MODES), F32), pltpu.VMEM((2, n_batch, S5_MODES), F32)]
    wspecs = [_resident((W_S5, 2 * S5_MODES)), _resident((2 * S5_MODES, W_S5)), _resident((2, S5_MODES))]
    yf = pl.pallas_call(
        functools.partial(_s5_fwd_kernel, n_batch=n_batch),
        grid=(nt,),
        in_specs=[pl.BlockSpec((tile_rows, W_S5), fwd_tile)] + wspecs,
        out_specs=pl.BlockSpec((tile_rows, W_S5), fwd_tile),
        out_shape=jax.ShapeDtypeStruct((rows_total, W_S5), F32),
        scratch_shapes=scratch,
        compiler_params=_params(("arbitrary",), V7X_VMEM_LIMIT_BYTES),
        name="s5_forward",
    )(u_rows, prm["bblk"][0], prm["cblk"], prm["lam"][0])
    return pl.pallas_call(
        functools.partial(_s5_bwd_kernel, n_batch=n_batch),
        grid=(nt,),
        in_specs=[pl.BlockSpec((tile_rows, W_S5), bwd_tile)] + wspecs + [
            pl.BlockSpec((tile_rows, W_S5), bwd_tile),
            _resident((1, W_S5)),
            _resident((W_S5, W_S5)),
            _resident((1, W_S5)),
        ],
        out_specs=pl.BlockSpec((tile_rows, W_S5), bwd_tile),
        out_shape=jax.ShapeDtypeStruct((rows_total, W_S5), BF16),
        scratch_shapes=scratch,
        compiler_params=_params(("arbitrary",), V7X_VMEM_LIMIT_BYTES),
        name="s5_backward",
    )(u_rows, prm["bblk"][1], prm["cblk"], prm["lam"][1], yf, prm["dsk"], prm["wglu"], prm["bglu"])


def _s5_prep(a_re, a_im, log_dt, b_re, b_im, c_re, c_im, d_skip, glu_w, glu_b):
    eye = jnp.eye(S5_GROUPS, dtype=F32)
    bblk, lam = [], []
    for d in range(2):
        dt = jnp.exp(log_dt[d])[:, None]
        dre, dim = dt * a_re[d], dt * a_im[d]
        mag = jnp.exp(dre)
        lam_re, lam_im = mag * jnp.cos(dim), mag * jnp.sin(dim)
        den = a_re[d] * a_re[d] + a_im[d] * a_im[d]
        coef_re = ((lam_re - 1.0) * a_re[d] + lam_im * a_im[d]) / den
        coef_im = (lam_im * a_re[d] - (lam_re - 1.0) * a_im[d]) / den
        cr, ci = coef_re[..., None], coef_im[..., None]
        bb_re = cr * b_re - ci * b_im
        bb_im = cr * b_im + ci * b_re
        blk = [jnp.einsum("gpc,gh->gchp", bb, eye).reshape(W_S5, S5_MODES) for bb in (bb_re, bb_im)]
        bblk.append(jnp.concatenate(blk, axis=1).astype(BF16))
        lam.append(jnp.stack([lam_re.reshape(-1), lam_im.reshape(-1)]))
    cblk = [jnp.einsum("gcp,gh->gphc", cc, eye).reshape(S5_MODES, W_S5) for cc in (c_re, -c_im)]
    return dict(bblk=bblk, lam=lam, cblk=jnp.concatenate(cblk, axis=0).astype(BF16),
                dsk=d_skip.reshape(1, W_S5), wglu=glu_w.astype(BF16), bglu=glu_b.reshape(1, W_S5))


def _head_blocks(cols):
    shape = (cols[0].shape[0], LANES)
    first = lax.broadcasted_iota(jnp.int32, shape, 1) < ML_HEAD_DIM
    return jnp.concatenate([jnp.where(first, cols[0], cols[1]), jnp.where(first, cols[2], cols[3])], axis=1)


def _split3(x):
    hi = x.astype(BF16)
    rem = x - hi.astype(F32)
    mid = rem.astype(BF16)
    return hi, mid, (rem - mid.astype(F32)).astype(BF16)


def _select_cols(x, sel, terms=3):
    return sum(_dot(part, sel) for part in _split3(x)[:terms])


def _mlstm_chunk(bb, qkv_ref, kmt_ref, gm_ref, gmt_ref, brow_ref, bcol_ref, ct_scr, n_scr, mc_scr, mr_scr,
                 *, reverse):
    n = ML_CHUNK
    g0 = 2 * ML_HEADS * (1 if reverse else 0)
    last = 0 if reverse else n - 1
    q = qkv_ref[bb, :, 0:W_ML]
    k = qkv_ref[bb, :, W_ML:2 * W_ML]
    v = qkv_ref[bb, :, 2 * W_ML:3 * W_ML]
    ti = lax.broadcasted_iota(jnp.int32, (n, n), 0)
    si = lax.broadcasted_iota(jnp.int32, (n, n), 1)
    causal = (si >= ti) if reverse else (si <= ti)
    tri = jnp.where(causal, 1.0, 0.0).astype(BF16)
    tri_t = jnp.where((ti >= si) if reverse else (ti <= si), 1.0, 0.0).astype(BF16)

    g_col = gm_ref[bb] + brow_ref[...]
    lf_col = _log_sigmoid(pltpu.roll(g_col, LANES - ML_HEADS, 1))
    bcum = sum(_dot(tri, part) for part in _split3(lf_col))
    r_col = g_col - bcum
    rowi = lax.broadcasted_iota(jnp.int32, (n, LANES), 0)
    cmax = r_col
    sh = 1
    while sh < n:
        if reverse:
            cmax = jnp.where(rowi < n - sh, jnp.maximum(cmax, pltpu.roll(cmax, n - sh, 0)), cmax)
        else:
            cmax = jnp.where(rowi >= sh, jnp.maximum(cmax, pltpu.roll(cmax, sh, 0)), cmax)
        sh *= 2
    m_col = mc_scr[bb, 0:1, :]
    mm = jnp.maximum(m_col, cmax)
    mm_last = mm[last:last + 1, :]
    mc_scr[bb, 0:1, :] = bcum[last:last + 1, :] + mm_last
    used = jnp.logical_and(lax.broadcasted_iota(jnp.int32, (n, LANES), 1) >= g0,
                           lax.broadcasted_iota(jnp.int32, (n, LANES), 1) < g0 + ML_HEADS)

    src = lax.broadcasted_iota(jnp.int32, (LANES, W_ML), 0)
    dst = lax.broadcasted_iota(jnp.int32, (LANES, W_ML), 1) // ML_HEAD_DIM
    sel_head = jnp.where(src == g0 + dst, 1.0, 0.0).astype(BF16)
    wi_b = _select_cols(jnp.where(used, jnp.exp(m_col - mm), 0.0), sel_head, terms=2)
    floor_b = jnp.exp(_select_cols(jnp.where(used, -(bcum + mm), 0.0), sel_head))

    g_row = gmt_ref[bb, g0:g0 + 2 * ML_HEADS, :] + bcol_ref[g0:g0 + 2 * ML_HEADS, 0:1]
    lf_row = _log_sigmoid(g_row)
    bcum_row = sum(_dot(part, tri_t) for part in _split3(lf_row))
    r_row = g_row[0:ML_HEADS] - bcum_row[ML_HEADS:2 * ML_HEADS]
    m_row = mr_scr[bb, 0:ML_HEADS, :]
    mm_row = jnp.maximum(m_row, jnp.max(r_row, axis=1, keepdims=True))
    decay_row = jnp.exp(m_row - mm_row)
    mr_scr[bb, 0:ML_HEADS, :] = jnp.sum(lf_row[ML_HEADS:2 * ML_HEADS], axis=1, keepdims=True) + mm_row
    w_state = jnp.exp(r_row - mm_row[:, 0:1])
    decay_col = jnp.concatenate(
        [jnp.broadcast_to(decay_row[h:h + 1, :], (ML_HEAD_DIM, LANES)) for h in range(ML_HEADS)], axis=0)
    w_state_rows = jnp.concatenate(
        [jnp.broadcast_to(w_state[h:h + 1, :], (ML_HEAD_DIM, n)) for h in range(ML_HEADS)], axis=0)

    lane_head = lax.broadcasted_iota(jnp.int32, (n, W_ML), 1) // ML_HEAD_DIM
    head_mask = [jnp.where(lane_head == h, 1.0, 0.0).astype(BF16) for h in range(ML_HEADS)]
    kexp = jnp.concatenate([k * mk for mk in head_mask], axis=0)
    vexp = jnp.concatenate([jnp.concatenate([v * mk, mk], axis=1) for mk in head_mask], axis=0)
    s_all = _dot_nt(q, kexp)
    w_list = []
    for h in range(ML_HEADS):
        mm_b = jnp.broadcast_to(mm[:, g0 + h:g0 + h + 1], (n, n))
        arg = jnp.where(causal, r_row[h:h + 1, :] - mm_b, NEG_BIG)
        w_list.append((jnp.exp(arg) * s_all[:, h * n:(h + 1) * n]).astype(BF16))
    intra = _dot(jnp.concatenate(w_list, axis=1), vexp)

    ct = ct_scr[bb]
    nst = n_scr[bb]
    bi = lax.broadcasted_iota(jnp.int32, (W_ML, W_ML), 0) // ML_HEAD_DIM
    bj = lax.broadcasted_iota(jnp.int32, (W_ML, W_ML), 1) // ML_HEAD_DIM
    same_head = bi == bj
    nblk = jnp.where(same_head, jnp.concatenate([nst, nst], axis=1), 0.0)
    num = wi_b * _dot(q, ct.astype(BF16)) + intra[:, 0:W_ML]
    den = wi_b * _dot(q, nblk.astype(BF16)) + intra[:, W_ML:2 * W_ML]
    h_out = num / jnp.maximum(jnp.abs(den), floor_b)

    ktw = (kmt_ref[bb].astype(F32) * w_state_rows).astype(BF16)
    vext = jnp.concatenate([v, jnp.ones((n, LANES), BF16)], axis=1)
    upd = _dot(ktw, vext)
    ct_scr[bb] = jnp.concatenate([decay_col, decay_col], axis=1) * ct + jnp.where(same_head, upd[:, 0:W_ML], 0.0)
    n_scr[bb] = decay_col * nst + upd[:, W_ML:W_ML + LANES]
    return h_out


def _mlstm_reset(ct_scr, n_scr, mc_scr, mr_scr):
    @pl.when(pl.program_id(1) == 0)
    def _():
        ct_scr[...] = jnp.zeros_like(ct_scr)
        n_scr[...] = jnp.zeros_like(n_scr)
        mc_scr[...] = jnp.zeros_like(mc_scr)
        mr_scr[...] = jnp.zeros_like(mr_scr)


def _mlstm_fwd_kernel(qkv_ref, kmt_ref, gm_ref, gmt_ref, brow_ref, bcol_ref, hf_ref, *state):
    _mlstm_reset(*state)
    for bb in range(ML_BATCH):
        hf_ref[bb] = _mlstm_chunk(bb, qkv_ref, kmt_ref, gm_ref, gmt_ref, brow_ref, bcol_ref, *state, reverse=False)


def _mlstm_bwd_kernel(qkv_ref, kmt_ref, gm_ref, gmt_ref, brow_ref, bcol_ref, hf_ref, om_ref, ng_ref, o_ref, *state):
    _mlstm_reset(*state)
    lane_head = lax.broadcasted_iota(jnp.int32, (ML_CHUNK, W_ML), 1) // ML_HEAD_DIM
    for bb in range(ML_BATCH):
        ht = hf_ref[bb] + _mlstm_chunk(bb, qkv_ref, kmt_ref, gm_ref, gmt_ref, brow_ref, bcol_ref, *state,
                                       reverse=True)
        sq = ht * ht
        ms = [jnp.sum(jnp.where(lane_head == h, sq, 0.0), axis=1, keepdims=True) * (1.0 / ML_HEAD_DIM)
              for h in range(ML_HEADS)]
        hn = ht * lax.rsqrt(_head_blocks(ms) + NORM_EPS) * ng_ref[...]
        o_ref[bb] = (hn * _sigmoid(om_ref[bb])).astype(BF16)


def _mlstm_calls(qkvm, kmt, gm, gmt, om, bias_row, bias_col, norm_g, n_tok, n_ctx):
    b, tc, _ = qkvm.shape
    nt = tc // ML_CHUNK
    n_lat, nct = n_tok // ML_CHUNK, n_ctx // ML_CHUNK
    nb = ML_BATCH
    state = [
        pltpu.VMEM((nb, W_ML, W_ML), F32),
        pltpu.VMEM((nb, W_ML, LANES), F32),
        pltpu.VMEM((nb, 8, LANES), F32),
        pltpu.VMEM((nb, 8, LANES), F32),
    ]

    def specs(reverse):
        def tile(g, s):
            return (g, _scan_tile(s, n_lat, nct, reverse), 0)

        def tile_t(g, s):
            return (g, 0, _scan_tile(s, n_lat, nct, reverse))

        row = lambda w: pl.BlockSpec((nb, ML_CHUNK, w), tile)
        common = [row(3 * W_ML), pl.BlockSpec((nb, W_ML, ML_CHUNK), tile_t), row(LANES),
                  pl.BlockSpec((nb, N_GATES, ML_CHUNK), tile_t), _resident((1, LANES)), _resident((N_GATES, LANES))]
        return common, row

    common, row = specs(False)
    hf = pl.pallas_call(
        _mlstm_fwd_kernel,
        grid=(b // nb, nt),
        in_specs=common,
        out_specs=row(W_ML),
        out_shape=jax.ShapeDtypeStruct((b, tc, W_ML), F32),
        scratch_shapes=state,
        compiler_params=_params(("parallel", "arbitrary"), V7X_VMEM_LIMIT_BYTES),
        name="mlstm_forward",
    )(qkvm, kmt, gm, gmt, bias_row, bias_col)
    common, row = specs(True)
    return pl.pallas_call(
        _mlstm_bwd_kernel,
        grid=(b // nb, nt),
        in_specs=common + [row(W_ML), row(W_ML), _resident((1, W_ML))],
        out_specs=row(W_ML),
        out_shape=jax.ShapeDtypeStruct((b, tc, W_ML), BF16),
        scratch_shapes=state,
        compiler_params=_params(("parallel", "arbitrary"), V7X_VMEM_LIMIT_BYTES),
        name="mlstm_backward",
    )(qkvm, kmt, gm, gmt, bias_row, bias_col, hf, om, norm_g.reshape(1, W_ML))


def _merge_kernel(ya_ref, ys_ref, ym_ref, gb_ref, x_ref, mod_ref, g_ref, wa_ref, ws_ref, wm_ref, wo_ref,
                  x1_ref, h2_ref, *, d_model):
    d = d_model
    mod = mod_ref[0]
    y = (_sigmoid(gb_ref[0, :, 0:d].astype(F32)) * _dot(ya_ref[0], wa_ref[...])
         + _sigmoid(gb_ref[0, :, d:2 * d].astype(F32)) * _dot(ys_ref[0], ws_ref[...])
         + _sigmoid(gb_ref[0, :, 2 * d:3 * d].astype(F32)) * _dot(ym_ref[0], wm_ref[...]))
    x1 = x_ref[0] + mod[:, 2 * d:3 * d] * _dot(y.astype(BF16), wo_ref[...])
    x1_ref[0] = x1
    h2_ref[0] = (_rms(x1) * g_ref[...] * (1.0 + mod[:, 4 * d:5 * d]) + mod[:, 3 * d:4 * d]).astype(BF16)


def _merge_call(ya, ys, ym, gb, x, modsel, mod_off, norm_g, wa, ws, wm, wo, tile, blk_off):
    b, n, d = x.shape

    def own(bi, i):
        return (bi, i, 0)

    def shared(bi, i):
        return (bi, i + blk_off, 0)

    return pl.pallas_call(
        functools.partial(_merge_kernel, d_model=d),
        grid=(b, n // tile),
        in_specs=[
            pl.BlockSpec((1, tile, W_ATTN), own),
            pl.BlockSpec((1, tile, W_S5), shared),
            pl.BlockSpec((1, tile, W_ML), shared),
            pl.BlockSpec((1, tile, N_BRANCH * d), shared),
            pl.BlockSpec((1, tile, d), own),
            pl.BlockSpec((1, 1, 6 * d), lambda bi, i: (mod_off + bi, 0, 0)),
            _resident((1, d)),
            _resident((W_ATTN, d)),
            _resident((W_S5, d)),
            _resident((W_ML, d)),
            _resident((d, d)),
        ],
        out_specs=[pl.BlockSpec((1, tile, d), own), pl.BlockSpec((1, tile, d), own)],
        out_shape=(jax.ShapeDtypeStruct((b, n, d), F32), jax.ShapeDtypeStruct((b, n, d), BF16)),
        compiler_params=_params(("parallel", "parallel"), V7X_VMEM_LIMIT_BYTES),
        name="merge",
    )(ya, ys, ym, gb, x, modsel, norm_g.reshape(1, d), wa, ws, wm, wo)


def _ffn_kernel(hp_ref, hm_ref, hn_ref, x1_ref, mod_ref, wup_ref, cw_ref, cb_ref, wdn_ref, fg_ref, o_ref,
                lhs_scr, act_scr, *, d_model, d_ff, tile, final):
    d = d_model
    halo = BF16_SUBLANES
    rows = tile + 2 * halo
    i = pl.program_id(1)
    prev_ok = i >= 1
    next_ok = i < pl.num_programs(1) - 1
    lhs_scr[0:halo] = jnp.where(prev_ok, hp_ref[0], jnp.zeros_like(hp_ref[0]))
    lhs_scr[halo:halo + tile] = hm_ref[0]
    lhs_scr[halo + tile:rows] = jnp.where(next_ok, hn_ref[0], jnp.zeros_like(hn_ref[0]))
    lhs = lhs_scr[...]
    for j in range(d_ff // FFN_CHUNK):
        parts = []
        for lo in (j * FFN_CHUNK, d_ff + j * FFN_CHUNK):
            u = _dot(lhs, wup_ref[:, lo:lo + FFN_CHUNK])
            cw = cw_ref[:, lo:lo + FFN_CHUNK]
            parts.append(pltpu.roll(u, 1, 0)[halo:halo + tile] * cw[0:1]
                         + u[halo:halo + tile] * cw[1:2]
                         + pltpu.roll(u, rows - 1, 0)[halo:halo + tile] * cw[2:3]
                         + cb_ref[:, lo:lo + FFN_CHUNK])
        a, v = parts
        act_scr[:, j * FFN_CHUNK:(j + 1) * FFN_CHUNK] = (a * _sigmoid(a) * v).astype(BF16)
    out = x1_ref[0] + mod_ref[0][:, 5 * d:6 * d] * _dot(act_scr[...], wdn_ref[...])
    if final:
        out = _rms(out) * fg_ref[...]
    o_ref[0] = out


def _ffn_call(h2, x1, modsel, mod_off, w_up, conv_w, conv_b, w_down, final_g, tile, *, final):
    b, n, d = x1.shape
    d_ff = w_down.shape[0]
    halo = BF16_SUBLANES
    per = tile // halo
    n_halo = n // halo
    return pl.pallas_call(
        functools.partial(_ffn_kernel, d_model=d, d_ff=d_ff, tile=tile, final=final),
        grid=(b, n // tile),
        in_specs=[
            pl.BlockSpec((1, halo, d), lambda bi, i: (bi, jnp.maximum(i * per - 1, 0), 0)),
            pl.BlockSpec((1, tile, d), lambda bi, i: (bi, i, 0)),
            pl.BlockSpec((1, halo, d), lambda bi, i: (bi, jnp.minimum((i + 1) * per, n_halo - 1), 0)),
            pl.BlockSpec((1, tile, d), lambda bi, i: (bi, i, 0)),
            pl.BlockSpec((1, 1, 6 * d), lambda bi, i: (mod_off + bi, 0, 0)),
            _resident((d, 2 * d_ff)),
            _resident((3, 2 * d_ff)),
            _resident((1, 2 * d_ff)),
            _resident((d_ff, d)),
            _resident((1, d)),
        ],
        out_specs=pl.BlockSpec((1, tile, d), lambda bi, i: (bi, i, 0)),
        out_shape=jax.ShapeDtypeStruct((b, n, d), F32),
        scratch_shapes=[pltpu.VMEM((tile + 2 * halo, d), BF16), pltpu.VMEM((tile, d_ff), BF16)],
        compiler_params=_params(("parallel", "parallel"), V7X_VMEM_LIMIT_BYTES),
        name="conv_ffn",
    )(h2, h2, h2, x1, modsel, w_up, conv_w, conv_b.reshape(1, 2 * d_ff), w_down, final_g.reshape(1, d))


def _prep_w_in(w):
    d = w.shape[0]
    k0, k1 = w[:, W_ATTN:W_ATTN + HEAD_DIM], w[:, W_ATTN + HEAD_DIM:W_ATTN + 2 * HEAD_DIM]
    o = W_ATTN + 2 * HEAD_DIM
    v0, v1 = w[:, o:o + HEAD_DIM], w[:, o + HEAD_DIM:o + 2 * HEAD_DIM]
    o += 2 * HEAD_DIM
    rest = w[:, o:o + W_S5 + 4 * W_ML]
    o += W_S5 + 4 * W_ML
    gm = w[:, o:o + N_GATES]
    gb = w[:, o + N_GATES:o + N_GATES + N_BRANCH * d]
    km = rest[:, W_S5 + W_ML:W_S5 + 2 * W_ML]
    cols = [w[:, 0:W_ATTN], k0, k0, k1, k1, v0, v0, v1, v1, rest, gb, jnp.tile(gm, (1, LANES // N_GATES))]
    w_t = jnp.concatenate([gm, km], axis=1).T
    return jnp.concatenate(cols, axis=1).astype(BF16), w_t.astype(BF16)


def _rope_tables(n_tok):
    pos = jnp.arange(n_tok)
    row = (pos // GRID_W).astype(F32)
    col = (pos % GRID_W).astype(F32)
    n_freq = HEAD_DIM // 4
    inv_freq = ROPE_BASE ** (-jnp.arange(n_freq, dtype=F32) / n_freq)
    lane = jnp.arange(LANES)
    dd = lane % HEAD_DIM
    by_col = (dd // (HEAD_DIM // 2)) == 1
    d32 = dd % (HEAD_DIM // 2)
    first = d32 < n_freq
    freq = inv_freq[d32 % n_freq]
    ang = jnp.where(by_col[None, :], col[:, None], row[:, None]) * freq[None, :]
    cos, sin = jnp.cos(ang), jnp.sin(ang)
    return cos, jnp.where(first[None, :], -sin, 0.0), jnp.where(first[None, :], 0.0, sin)


def kernel(x, c, ctx, c_ctx, mod_w, mod_b, norm1_g, norm2_g, w_in, attn_sink, s5_a_re, s5_a_im, s5_log_dt, s5_b_re, s5_b_im, s5_c_re, s5_c_im, s5_d, s5_glu_w, s5_glu_b, ml_igate_b, ml_fgate_b, ml_norm_g, w_branch_attn, w_branch_s5, w_branch_ml, w_out, ffn_w_up, ffn_conv_w, ffn_conv_b, ffn_w_down, final_norm_g):
    b, n_tok, d = x.shape
    n_ctx = ctx.shape[1]
    n_layer = mod_w.shape[0]
    assert n_tok % ROW_TILE == 0 and n_tok % FFN_TILE == 0 and n_tok % n_ctx == 0 and n_tok % GRID_W == 0
    assert n_ctx % ATTN_TILE == 0 and n_ctx % ML_CHUNK == 0 and n_ctx % S5_TILE == 0
    assert b % 8 == 0 and b % ML_BATCH == 0 and b + 1 <= 16 and d % LANES == 0
    tc = n_tok + n_ctx

    c_all = jnp.zeros((16, d), F32).at[:b].set(c).at[b].set(c_ctx)
    mod = _mod_call(c_all, mod_w.astype(BF16), mod_b)
    rope = _rope_tables(n_tok)
    xl, xk = x, ctx

    for l in range(n_layer):
        final = l == n_layer - 1
        modsel = jnp.concatenate([jnp.broadcast_to(mod[l, b:b + 1], (b, 6 * d)), mod[l, :b]]).reshape(2 * b, 1, 6 * d)
        w_in_l, w_t = _prep_w_in(w_in[l])
        outs = _in_call(xl, modsel, b, norm1_g[l], w_in_l, w_t, tc, ROW_TILE, 0, rope=rope)
        outs = _in_call(xk, modsel, 0, norm1_g[l], w_in_l, w_t, tc, n_ctx, n_tok // n_ctx, prev=outs)
        qa, kv, us, qkvm, om, gb, gm, gmt, kmt = outs

        ya, ya_ctx = _attn_calls(attn_sink[l], qa, kv, n_tok, n_ctx, with_ctx=not final)

        s5p = _s5_prep(s5_a_re[l], s5_a_im[l], s5_log_dt[l], s5_b_re[l], s5_b_im[l], s5_c_re[l], s5_c_im[l],
                       s5_d[l], s5_glu_w[l], s5_glu_b[l])
        u_rows = us.transpose(1, 0, 2).reshape(tc * b, W_S5)
        ys = _s5_calls(u_rows, s5p, b, n_tok, n_ctx).reshape(tc, b, W_S5).transpose(1, 0, 2)

        gate_b = jnp.concatenate([ml_igate_b[l, 0], ml_fgate_b[l, 0], ml_igate_b[l, 1], ml_fgate_b[l, 1]])
        bias_row = jnp.tile(gate_b, LANES // N_GATES).reshape(1, LANES)
        bias_col = jnp.broadcast_to(gate_b[:, None], (N_GATES, LANES))
        ym = _mlstm_calls(qkvm, kmt, gm, gmt, om, bias_row, bias_col, ml_norm_g[l], n_tok, n_ctx)

        wb = [w.astype(BF16) for w in (w_branch_attn[l], w_branch_s5[l], w_branch_ml[l], w_out[l])]
        wf = (ffn_w_up[l].astype(BF16), ffn_conv_w[l], ffn_conv_b[l], ffn_w_down[l].astype(BF16), final_norm_g)
        x1, h2 = _merge_call(ya, ys, ym, gb, xl, modsel, b, norm2_g[l], *wb, ROW_TILE, 0)
        xl = _ffn_call(h2, x1, modsel, b, *wf, FFN_TILE, final=final)
        if not final:
            x1, h2 = _merge_call(ya_ctx, ys, ym, gb, xk, modsel, 0, norm2_g[l], *wb, n_ctx, n_tok // n_ctx)
            xk = _ffn_call(h2, x1, modsel, 0, *wf, n_ctx, final=False)
    return xl
```

```python
import functools
import math

import jax
import jax.numpy as jnp
from jax import lax
from jax.experimental import pallas as pl
from jax.experimental.pallas import tpu as pltpu

F32 = jnp.float32
BF16 = jnp.bfloat16

N_HEADS = 8
N_KV_HEADS = 2
HEAD_DIM = 64
Q_PER_KV = N_HEADS // N_KV_HEADS
W_ATTN = N_HEADS * HEAD_DIM
WINDOW = 128
GRID_W = 64
ROPE_BASE = 10000.0
W_S5 = 256
S5_GROUP = 16
S5_GROUPS = W_S5 // S5_GROUP
S5_STATE = 64
S5_MODES = S5_GROUPS * S5_STATE
ML_HEADS = 4
ML_HEAD_DIM = 64
W_ML = ML_HEADS * ML_HEAD_DIM
N_GATES = 4 * ML_HEADS
N_BRANCH = 3
NORM_EPS = 1e-6

LANES = 128
BF16_SUBLANES = 16
V7X_VMEM_LIMIT_BYTES = 56 * 1024 * 1024

ROW_TILE = 512
FFN_TILE = 1024
ATTN_TILE = 256
ML_CHUNK = 256
ML_BATCH = 4
S5_TILE = 128
FFN_CHUNK = 256
NEG_BIG = -1e30
LOG2E = math.log2(math.e)
Q_SCALE = LOG2E * HEAD_DIM ** -0.5

Z_QA = 0
Z_KV = Z_QA + W_ATTN
Z_US = Z_KV + 4 * LANES
Z_QKVM = Z_US + W_S5
Z_OM = Z_QKVM + 3 * W_ML
Z_GB = Z_OM + W_ML


def _sigmoid(x):
    return 1.0 / (1.0 + jnp.exp(-x))


def _log_sigmoid(x):
    return jnp.minimum(x, 0.0) - jnp.log(1.0 + jnp.exp(-jnp.abs(x)))


def _gelu_tanh(x):
    return 0.5 * x * (1.0 + jnp.tanh(math.sqrt(2.0 / math.pi) * (x + 0.044715 * (x * x * x))))


def _rms(x):
    return x * lax.rsqrt(jnp.mean(x * x, axis=-1, keepdims=True) + NORM_EPS)


def _dot(a, b):
    return jnp.dot(a, b, preferred_element_type=F32)


def _dot_nt(a, b):
    return lax.dot_general(a, b, (((1,), (1,)), ((), ())), preferred_element_type=F32)


def _params(sem, vmem=None):
    return pltpu.CompilerParams(dimension_semantics=sem, vmem_limit_bytes=vmem)


def _resident(shape):
    return pl.BlockSpec(shape, lambda *_: (0,) * len(shape), pipeline_mode=pl.Buffered(1))


def _mod_kernel(c_ref, w_ref, b_ref, o_ref):
    c = c_ref[...]
    s = (c * _sigmoid(c)).astype(BF16)
    o_ref[0] = _dot(s, w_ref[0]) + b_ref[0]


def _mod_call(c_all, mod_w, mod_b):
    n_layer, d, d6 = mod_w.shape
    tn = d6 // 4
    rows = c_all.shape[0]
    return pl.pallas_call(
        _mod_kernel,
        grid=(n_layer, d6 // tn),
        in_specs=[
            pl.BlockSpec((rows, d), lambda l, j: (0, 0)),
            pl.BlockSpec((1, d, tn), lambda l, j: (l, 0, j)),
            pl.BlockSpec((1, 1, tn), lambda l, j: (l, 0, j)),
        ],
        out_specs=pl.BlockSpec((1, rows, tn), lambda l, j: (l, 0, j)),
        out_shape=jax.ShapeDtypeStruct((n_layer, rows, d6), F32),
        compiler_params=_params(("parallel", "parallel")),
        name="modulation",
    )(c_all, mod_w, mod_b.reshape(n_layer, 1, d6))


def _rope(xb, cos, s1, s2):
    return xb * cos + pltpu.roll(xb, LANES - 16, 1) * s1 + pltpu.roll(xb, 16, 1) * s2


N_IN_OUT = 9


def _in_kernel(x_ref, mod_ref, g_ref, w_ref, wt_ref, *rest, d_model, rotary):
    d = d_model
    if rotary:
        cos, s1, s2 = (r[...] for r in rest[:3])
        rest = rest[3:]
    qa_ref, kv_ref, us_ref, qkvm_ref, om_ref, gb_ref, gm_ref, gmt_ref, kmt_ref = rest[-N_IN_OUT:]
    k_scale = ML_HEAD_DIM ** -0.5
    mod = mod_ref[0]
    h = (_rms(x_ref[0]) * g_ref[...] * (1.0 + mod[:, d:2 * d]) + mod[:, 0:d]).astype(BF16)

    def proj(lo, width):
        return _dot(h, w_ref[:, lo:lo + width])

    def rot(blk):
        return _rope(blk, cos, s1, s2) if rotary else blk

    zq = proj(Z_QA, W_ATTN)
    for p in range(W_ATTN // LANES):
        sl = slice(p * LANES, (p + 1) * LANES)
        qa_ref[0, :, sl] = (rot(zq[:, sl]) * Q_SCALE).astype(BF16)
    zkv = proj(Z_KV, 4 * LANES)
    for p in range(4):
        sl = slice(p * LANES, (p + 1) * LANES)
        kv_ref[0, :, sl] = (rot(zkv[:, sl]) if p < 2 else zkv[:, sl]).astype(BF16)
    us_ref[0] = proj(Z_US, W_S5)
    zm = proj(Z_QKVM, 3 * W_ML)
    qkvm_ref[0, :, 0:W_ML] = zm[:, 0:W_ML].astype(BF16)
    qkvm_ref[0, :, W_ML:2 * W_ML] = (zm[:, W_ML:2 * W_ML] * k_scale).astype(BF16)
    qkvm_ref[0, :, 2 * W_ML:3 * W_ML] = zm[:, 2 * W_ML:3 * W_ML].astype(BF16)
    om_ref[0] = proj(Z_OM, W_ML)
    for p in range(N_BRANCH):
        gb_ref[0, :, p * d:(p + 1) * d] = proj(Z_GB + p * d, d).astype(BF16)
    gm_ref[0] = proj(Z_GB + N_BRANCH * d, LANES)
    zt = _dot_nt(wt_ref[...], h)
    gmt_ref[0] = zt[0:N_GATES]
    kmt_ref[0] = (zt[N_GATES:N_GATES + W_ML] * k_scale).astype(BF16)


def _in_call(x, modsel, mod_off, norm_g, w_in, w_t, tc, tile, blk_off, rope=None, prev=None):
    b, n, d = x.shape
    nz = w_in.shape[1]
    widths = (W_ATTN, 4 * LANES, W_S5, 3 * W_ML, W_ML, N_BRANCH * d, LANES)
    dtypes = (BF16, BF16, F32, BF16, F32, BF16, F32)
    out_shapes = [jax.ShapeDtypeStruct((b, tc, w), t) for w, t in zip(widths, dtypes)]
    out_shapes.append(jax.ShapeDtypeStruct((b, N_GATES, tc), F32))
    out_shapes.append(jax.ShapeDtypeStruct((b, W_ML, tc), BF16))
    out_specs = [pl.BlockSpec((1, tile, w), lambda bi, i: (bi, i + blk_off, 0)) for w in widths]
    out_specs.append(pl.BlockSpec((1, N_GATES, tile), lambda bi, i: (bi, 0, i + blk_off)))
    out_specs.append(pl.BlockSpec((1, W_ML, tile), lambda bi, i: (bi, 0, i + blk_off)))
    in_specs = [
        pl.BlockSpec((1, tile, d), lambda bi, i: (bi, i, 0)),
        pl.BlockSpec((1, 1, 6 * d), lambda bi, i: (mod_off + bi, 0, 0)),
        _resident((1, d)),
        _resident((d, nz)),
        _resident((N_GATES + W_ML, d)),
    ]
    args = [x, modsel, norm_g.reshape(1, d), w_in, w_t]
    if rope is not None:
        in_specs += [pl.BlockSpec((tile, LANES), lambda bi, i: (i, 0))] * 3
        args += list(rope)
    aliases = {}
    if prev is not None:
        aliases = {len(args) + k: k for k in range(N_IN_OUT)}
        in_specs += [pl.BlockSpec(memory_space=pl.ANY)] * N_IN_OUT
        args += list(prev)
    return pl.pallas_call(
        functools.partial(_in_kernel, d_model=d, rotary=rope is not None),
        grid=(b, n // tile),
        in_specs=in_specs,
        out_specs=out_specs,
        out_shape=out_shapes,
        input_output_aliases=aliases,
        compiler_params=_params(("parallel", "parallel"), V7X_VMEM_LIMIT_BYTES),
        name="in_proj",
    )(*args)


def _attn_heads(sink_ref, q_ref, kv_scr, o_ref, bias, n_key):
    lane = lax.broadcasted_iota(jnp.int32, (n_key, LANES), 1)
    n_q = q_ref.shape[1]
    own = lax.broadcasted_iota(jnp.int32, (n_q, LANES), 1) < HEAD_DIM
    first = lax.broadcasted_iota(jnp.int32, (2 * n_q, 1), 0) < n_q
    n_bias = 0 if bias is None else bias.shape[1]
    if n_bias:
        bias = jnp.concatenate([bias, bias], axis=0)
    for kvh in range(N_KV_HEADS):
        kd = kv_scr[0:n_key, kvh * LANES:(kvh + 1) * LANES]
        vd = kv_scr[0:n_key, (N_KV_HEADS + kvh) * LANES:(N_KV_HEADS + kvh + 1) * LANES]
        blocks = (2 * kvh, 2 * kvh + 1)
        qb = jnp.concatenate([q_ref[0, :, p * LANES:(p + 1) * LANES] for p in blocks], axis=0)
        outs = []
        for par in range(2):
            half = (lane >= HEAD_DIM) if par else (lane < HEAD_DIM)
            km = jnp.where(half, kd, jnp.zeros_like(kd))
            vm = jnp.where(half, vd, jnp.ones_like(vd))
            s = _dot_nt(qb, km)
            parts = [s[:, 0:n_bias] + bias, s[:, n_bias:n_key]] if n_bias else [s]
            sk = jnp.where(first, sink_ref[2 * blocks[0] + par], sink_ref[2 * blocks[1] + par]) * LOG2E
            m = sk
            for part in parts:
                m = jnp.maximum(m, jnp.max(part, axis=1, keepdims=True))
            e = jnp.concatenate([jnp.exp2(part - m).astype(BF16) for part in parts], axis=1)
            o = _dot(e, vm)
            den = pltpu.roll(o, HEAD_DIM, 1) + jnp.exp2(sk - m)
            outs.append(o / den)
        for j, p in enumerate(blocks):
            rows = slice(j * n_q, (j + 1) * n_q)
            o_ref[0, :, p * LANES:(p + 1) * LANES] = jnp.where(own, outs[0][rows], outs[1][rows]).astype(BF16)


def _attn_latent_kernel(sink_ref, q_ref, kvp_ref, kvm_ref, kvn_ref, kvc_ref, o_ref, kv_scr, *, n_tok, n_ctx):
    i = pl.program_id(1)
    n_loc = ATTN_TILE + 2 * WINDOW
    n_key = n_loc + n_ctx
    kv_scr[0:WINDOW] = kvp_ref[0]
    kv_scr[WINDOW:WINDOW + ATTN_TILE] = kvm_ref[0]
    kv_scr[WINDOW + ATTN_TILE:n_loc] = kvn_ref[0]
    kv_scr[n_loc:n_key] = kvc_ref[0]
    t0 = i * ATTN_TILE
    row = lax.broadcasted_iota(jnp.int32, (ATTN_TILE, n_loc), 0)
    col = lax.broadcasted_iota(jnp.int32, (ATTN_TILE, n_loc), 1)
    lo = jnp.maximum(row, WINDOW - t0)
    hi = jnp.minimum(row + 2 * WINDOW, n_tok - t0 + WINDOW - 1)
    valid = jnp.logical_and(col >= lo, col <= hi)
    _attn_heads(sink_ref, q_ref, kv_scr, o_ref, jnp.where(valid, 0.0, NEG_BIG), n_key)


def _attn_context_kernel(sink_ref, q_ref, kvc_ref, o_ref, kv_scr, *, n_ctx):
    kv_scr[...] = kvc_ref[0]
    _attn_heads(sink_ref, q_ref, kv_scr, o_ref, None, n_ctx)


def _attn_calls(sink, qa, kv, n_tok, n_ctx, with_ctx):
    b = qa.shape[0]
    kvw = kv.shape[2]
    per = ATTN_TILE // WINDOW
    ctx_blk = n_tok // n_ctx
    last_half = (n_tok + n_ctx) // WINDOW - 1
    smem = pl.BlockSpec(memory_space=pltpu.SMEM)
    ya = pl.pallas_call(
        functools.partial(_attn_latent_kernel, n_tok=n_tok, n_ctx=n_ctx),
        grid=(b, n_tok // ATTN_TILE),
        in_specs=[
            smem,
            pl.BlockSpec((1, ATTN_TILE, W_ATTN), lambda bi, i: (bi, i, 0)),
            pl.BlockSpec((1, WINDOW, kvw), lambda bi, i: (bi, jnp.maximum(per * i - 1, 0), 0)),
            pl.BlockSpec((1, ATTN_TILE, kvw), lambda bi, i: (bi, i, 0)),
            pl.BlockSpec((1, WINDOW, kvw), lambda bi, i: (bi, jnp.minimum(per * i + per, last_half), 0)),
            pl.BlockSpec((1, n_ctx, kvw), lambda bi, i: (bi, ctx_blk, 0)),
        ],
        out_specs=pl.BlockSpec((1, ATTN_TILE, W_ATTN), lambda bi, i: (bi, i, 0)),
        out_shape=jax.ShapeDtypeStruct((b, n_tok, W_ATTN), BF16),
        scratch_shapes=[pltpu.VMEM((ATTN_TILE + 2 * WINDOW + n_ctx, kvw), BF16)],
        compiler_params=_params(("parallel", "parallel"), V7X_VMEM_LIMIT_BYTES),
        name="attention",
    )(sink, qa, kv, kv, kv, kv)
    if not with_ctx:
        return ya, None
    ya_ctx = pl.pallas_call(
        functools.partial(_attn_context_kernel, n_ctx=n_ctx),
        grid=(b,),
        in_specs=[
            smem,
            pl.BlockSpec((1, n_ctx, W_ATTN), lambda bi: (bi, ctx_blk, 0)),
            pl.BlockSpec((1, n_ctx, kvw), lambda bi: (bi, ctx_blk, 0)),
        ],
        out_specs=pl.BlockSpec((1, n_ctx, W_ATTN), lambda bi: (bi, 0, 0)),
        out_shape=jax.ShapeDtypeStruct((b, n_ctx, W_ATTN), BF16),
        scratch_shapes=[pltpu.VMEM((n_ctx, kvw), BF16)],
        compiler_params=_params(("parallel",)),
        name="attention_ctx",
    )(sink, qa, kv)
    return ya, ya_ctx


def _scan_tile(step, n_lat, n_ctx_tiles, reverse):
    if reverse:
        return jnp.where(step < n_ctx_tiles, n_lat + n_ctx_tiles - 1 - step, n_lat - 1 - (step - n_ctx_tiles))
    return jnp.where(step < n_ctx_tiles, n_lat + step, step - n_ctx_tiles)


def _s5_scan_tile(u_ref, bblk_ref, cblk_ref, lam_ref, x_scr, st_scr, *, reverse, n_batch):
    i = pl.program_id(0)

    @pl.when(i == 0)
    def _():
        st_scr[...] = jnp.zeros_like(st_scr)

    u = u_ref[...]
    x_scr[...] = _dot(u.astype(BF16), bblk_ref[...])
    lre = jnp.broadcast_to(lam_ref[0:1, :], (n_batch, S5_MODES))
    lim = jnp.broadcast_to(lam_ref[1:2, :], (n_batch, S5_MODES))

    re, im = st_scr[0], st_scr[1]
    for j in range(S5_TILE):
        r0 = ((S5_TILE - 1 - j) if reverse else j) * n_batch
        bre = x_scr[r0:r0 + n_batch, 0:S5_MODES]
        bim = x_scr[r0:r0 + n_batch, S5_MODES:2 * S5_MODES]
        re, im = lre * re - lim * im + bre, lre * im + lim * re + bim
        x_scr[r0:r0 + n_batch, 0:S5_MODES] = re
        x_scr[r0:r0 + n_batch, S5_MODES:2 * S5_MODES] = im
    st_scr[0] = re
    st_scr[1] = im
    return u, _dot(x_scr[...].astype(BF16), cblk_ref[...])


def _s5_fwd_kernel(u_ref, bblk_ref, cblk_ref, lam_ref, o_ref, x_scr, st_scr, *, n_batch):
    _, y = _s5_scan_tile(u_ref, bblk_ref, cblk_ref, lam_ref, x_scr, st_scr, reverse=False, n_batch=n_batch)
    o_ref[...] = y


def _s5_bwd_kernel(u_ref, bblk_ref, cblk_ref, lam_ref, yf_ref, dsk_ref, wg_ref, bg_ref, o_ref, x_scr, st_scr,
                   *, n_batch):
    u, y = _s5_scan_tile(u_ref, bblk_ref, cblk_ref, lam_ref, x_scr, st_scr, reverse=True, n_batch=n_batch)
    y = _gelu_tanh(y + yf_ref[...] + dsk_ref[...] * u)
    o_ref[...] = (y * _sigmoid(_dot(y.astype(BF16), wg_ref[...]) + bg_ref[...])).astype(BF16)


def _s5_calls(u_rows, prm, n_batch, n_tok, n_ctx):
    rows_total = u_rows.shape[0]
    tile_rows = S5_TILE * n_batch
    nt = rows_total // tile_rows
    n_lat, nct = n_tok // S5_TILE, n_ctx // S5_TILE

    def fwd_tile(i):
        return (_scan_tile(i, n_lat, nct, False), 0)

    def bwd_tile(i):
        return (_scan_tile(i, n_lat, nct, True), 0)

    scratch = [pltpu.VMEM((tile_rows, 2 * S5_MODES), F32), pltpu.VMEM((2, n_batch, S5_MODES), F32)]
    wspecs = [_resident((W_S5, 2 * S5_MODES)), _resident((2 * S5_MODES, W_S5)), _resident((2, S5_MODES))]
    yf = pl.pallas_call(
        functools.partial(_s5_fwd_kernel, n_batch=n_batch),
        grid=(nt,),
        in_specs=[pl.BlockSpec((tile_rows, W_S5), fwd_tile)] + wspecs,
        out_specs=pl.BlockSpec((tile_rows, W_S5), fwd_tile),
        out_shape=jax.ShapeDtypeStruct((rows_total, W_S5), F32),
        scratch_shapes=scratch,
        compiler_params=_params(("arbitrary",), V7X_VMEM_LIMIT_BYTES),
        name="s5_forward",
    )(u_rows, prm["bblk"][0], prm["cblk"], prm["lam"][0])
    return pl.pallas_call(
        functools.partial(_s5_bwd_kernel, n_batch=n_batch),
        grid=(nt,),
        in_specs=[pl.BlockSpec((tile_rows, W_S5), bwd_tile)] + wspecs + [
            pl.BlockSpec((tile_rows, W_S5), bwd_tile),
            _resident((1, W_S5)),
            _resident((W_S5, W_S5)),
            _resident((1, W_S5)),
        ],
        out_specs=pl.BlockSpec((tile_rows, W_S5), bwd_tile),
        out_shape=jax.ShapeDtypeStruct((rows_total, W_S5), BF16),
        scratch_shapes=scratch,
        compiler_params=_params(("arbitrary",), V7X_VMEM_LIMIT_BYTES),
        name="s5_backward",
    )(u_rows, prm["bblk"][1], prm["cblk"], prm["lam"][1], yf, prm["dsk"], prm["wglu"], prm["bglu"])


def _s5_prep(a_re, a_im, log_dt, b_re, b_im, c_re, c_im, d_skip, glu_w, glu_b):
    eye = jnp.eye(S5_GROUPS, dtype=F32)
    bblk, lam = [], []
    for d in range(2):
        dt = jnp.exp(log_dt[d])[:, None]
        dre, dim = dt * a_re[d], dt * a_im[d]
        mag = jnp.exp(dre)
        lam_re, lam_im = mag * jnp.cos(dim), mag * jnp.sin(dim)
        den = a_re[d] * a_re[d] + a_im[d] * a_im[d]
        coef_re = ((lam_re - 1.0) * a_re[d] + lam_im * a_im[d]) / den
        coef_im = (lam_im * a_re[d] - (lam_re - 1.0) * a_im[d]) / den
        cr, ci = coef_re[..., None], coef_im[..., None]
        bb_re = cr * b_re - ci * b_im
        bb_im = cr * b_im + ci * b_re
        blk = [jnp.einsum("gpc,gh->gchp", bb, eye).reshape(W_S5, S5_MODES) for bb in (bb_re, bb_im)]
        bblk.append(jnp.concatenate(blk, axis=1).astype(BF16))
        lam.append(jnp.stack([lam_re.reshape(-1), lam_im.reshape(-1)]))
    cblk = [jnp.einsum("gcp,gh->gphc", cc, eye).reshape(S5_MODES, W_S5) for cc in (c_re, -c_im)]
    return dict(bblk=bblk, lam=lam, cblk=jnp.concatenate(cblk, axis=0).astype(BF16),
                dsk=d_skip.reshape(1, W_S5), wglu=glu_w.astype(BF16), bglu=glu_b.reshape(1, W_S5))


def _head_blocks(cols):
    shape = (cols[0].shape[0], LANES)
    first = lax.broadcasted_iota(jnp.int32, shape, 1) < ML_HEAD_DIM
    return jnp.concatenate([jnp.where(first, cols[0], cols[1]), jnp.where(first, cols[2], cols[3])], axis=1)


def _split3(x):
    hi = x.astype(BF16)
    rem = x - hi.astype(F32)
    mid = rem.astype(BF16)
    return hi, mid, (rem - mid.astype(F32)).astype(BF16)


def _select_cols(x, sel, terms=3):
    return sum(_dot(part, sel) for part in _split3(x)[:terms])


def _mlstm_chunk(bb, qkv_ref, kmt_ref, gm_ref, gmt_ref, brow_ref, bcol_ref, ct_scr, n_scr, mc_scr, mr_scr,
                 *, reverse):
    n = ML_CHUNK
    g0 = 2 * ML_HEADS * (1 if reverse else 0)
    last = 0 if reverse else n - 1
    q = qkv_ref[bb, :, 0:W_ML]
    k = qkv_ref[bb, :, W_ML:2 * W_ML]
    v = qkv_ref[bb, :, 2 * W_ML:3 * W_ML]
    ti = lax.broadcasted_iota(jnp.int32, (n, n), 0)
    si = lax.broadcasted_iota(jnp.int32, (n, n), 1)
    causal = (si >= ti) if reverse else (si <= ti)
    tri = jnp.where(causal, 1.0, 0.0).astype(BF16)
    tri_t = jnp.where((ti >= si) if reverse else (ti <= si), 1.0, 0.0).astype(BF16)

    g_col = gm_ref[bb] + brow_ref[...]
    lf_col = _log_sigmoid(pltpu.roll(g_col, LANES - ML_HEADS, 1))
    bcum = sum(_dot(tri, part) for part in _split3(lf_col))
    r_col = g_col - bcum
    rowi = lax.broadcasted_iota(jnp.int32, (n, LANES), 0)
    cmax = r_col
    sh = 1
    while sh < n:
        if reverse:
            cmax = jnp.where(rowi < n - sh, jnp.maximum(cmax, pltpu.roll(cmax, n - sh, 0)), cmax)
        else:
            cmax = jnp.where(rowi >= sh, jnp.maximum(cmax, pltpu.roll(cmax, sh, 0)), cmax)
        sh *= 2
    m_col = mc_scr[bb, 0:1, :]
    mm = jnp.maximum(m_col, cmax)
    mm_last = mm[last:last + 1, :]
    mc_scr[bb, 0:1, :] = bcum[last:last + 1, :] + mm_last
    used = jnp.logical_and(lax.broadcasted_iota(jnp.int32, (n, LANES), 1) >= g0,
                           lax.broadcasted_iota(jnp.int32, (n, LANES), 1) < g0 + ML_HEADS)

    src = lax.broadcasted_iota(jnp.int32, (LANES, W_ML), 0)
    dst = lax.broadcasted_iota(jnp.int32, (LANES, W_ML), 1) // ML_HEAD_DIM
    sel_head = jnp.where(src == g0 + dst, 1.0, 0.0).astype(BF16)
    wi_b = _select_cols(jnp.where(used, jnp.exp(m_col - mm), 0.0), sel_head, terms=2)
    floor_b = jnp.exp(_select_cols(jnp.where(used, -(bcum + mm), 0.0), sel_head))

    g_row = gmt_ref[bb, g0:g0 + 2 * ML_HEADS, :] + bcol_ref[g0:g0 + 2 * ML_HEADS, 0:1]
    lf_row = _log_sigmoid(g_row)
    bcum_row = sum(_dot(part, tri_t) for part in _split3(lf_row))
    r_row = g_row[0:ML_HEADS] - bcum_row[ML_HEADS:2 * ML_HEADS]
    m_row = mr_scr[bb, 0:ML_HEADS, :]
    mm_row = jnp.maximum(m_row, jnp.max(r_row, axis=1, keepdims=True))
    decay_row = jnp.exp(m_row - mm_row)
    mr_scr[bb, 0:ML_HEADS, :] = jnp.sum(lf_row[ML_HEADS:2 * ML_HEADS], axis=1, keepdims=True) + mm_row
    w_state = jnp.exp(r_row - mm_row[:, 0:1])
    decay_col = jnp.concatenate(
        [jnp.broadcast_to(decay_row[h:h + 1, :], (ML_HEAD_DIM, LANES)) for h in range(ML_HEADS)], axis=0)
    w_state_rows = jnp.concatenate(
        [jnp.broadcast_to(w_state[h:h + 1, :], (ML_HEAD_DIM, n)) for h in range(ML_HEADS)], axis=0)

    lane_head = lax.broadcasted_iota(jnp.int32, (n, W_ML), 1) // ML_HEAD_DIM
    head_mask = [jnp.where(lane_head == h, 1.0, 0.0).astype(BF16) for h in range(ML_HEADS)]
    kexp = jnp.concatenate([k * mk for mk in head_mask], axis=0)
    vexp = jnp.concatenate([jnp.concatenate([v * mk, mk], axis=1) for mk in head_mask], axis=0)
    s_all = _dot_nt(q, kexp)
    w_list = []
    for h in range(ML_HEADS):
        mm_b = jnp.broadcast_to(mm[:, g0 + h:g0 + h + 1], (n, n))
        arg = jnp.where(causal, r_row[h:h + 1, :] - mm_b, NEG_BIG)
        w_list.append((jnp.exp(arg) * s_all[:, h * n:(h + 1) * n]).astype(BF16))
    intra = _dot(jnp.concatenate(w_list, axis=1), vexp)

    ct = ct_scr[bb]
    nst = n_scr[bb]
    bi = lax.broadcasted_iota(jnp.int32, (W_ML, W_ML), 0) // ML_HEAD_DIM
    bj = lax.broadcasted_iota(jnp.int32, (W_ML, W_ML), 1) // ML_HEAD_DIM
    same_head = bi == bj
    nblk = jnp.where(same_head, jnp.concatenate([nst, nst], axis=1), 0.0)
    num = wi_b * _dot(q, ct.astype(BF16)) + intra[:, 0:W_ML]
    den = wi_b * _dot(q, nblk.astype(BF16)) + intra[:, W_ML:2 * W_ML]
    h_out = num / jnp.maximum(jnp.abs(den), floor_b)

    ktw = (kmt_ref[bb].astype(F32) * w_state_rows).astype(BF16)
    vext = jnp.concatenate([v, jnp.ones((n, LANES), BF16)], axis=1)
    upd = _dot(ktw, vext)
    ct_scr[bb] = jnp.concatenate([decay_col, decay_col], axis=1) * ct + jnp.where(same_head, upd[:, 0:W_ML], 0.0)
    n_scr[bb] = decay_col * nst + upd[:, W_ML:W_ML + LANES]
    return h_out


def _mlstm_reset(ct_scr, n_scr, mc_scr, mr_scr):
    @pl.when(pl.program_id(1) == 0)
    def _():
        ct_scr[...] = jnp.zeros_like(ct_scr)
        n_scr[...] = jnp.zeros_like(n_scr)
        mc_scr[...] = jnp.zeros_like(mc_scr)
        mr_scr[...] = jnp.zeros_like(mr_scr)


def _mlstm_fwd_kernel(qkv_ref, kmt_ref, gm_ref, gmt_ref, brow_ref, bcol_ref, hf_ref, *state):
    _mlstm_reset(*state)
    for bb in range(ML_BATCH):
        hf_ref[bb] = _mlstm_chunk(bb, qkv_ref, kmt_ref, gm_ref, gmt_ref, brow_ref, bcol_ref, *state, reverse=False)


def _mlstm_bwd_kernel(qkv_ref, kmt_ref, gm_ref, gmt_ref, brow_ref, bcol_ref, hf_ref, om_ref, ng_ref, o_ref, *state):
    _mlstm_reset(*state)
    lane_head = lax.broadcasted_iota(jnp.int32, (ML_CHUNK, W_ML), 1) // ML_HEAD_DIM
    for bb in range(ML_BATCH):
        ht = hf_ref[bb] + _mlstm_chunk(bb, qkv_ref, kmt_ref, gm_ref, gmt_ref, brow_ref, bcol_ref, *state,
                                       reverse=True)
        sq = ht * ht
        ms = [jnp.sum(jnp.where(lane_head == h, sq, 0.0), axis=1, keepdims=True) * (1.0 / ML_HEAD_DIM)
              for h in range(ML_HEADS)]
        hn = ht * lax.rsqrt(_head_blocks(ms) + NORM_EPS) * ng_ref[...]
        o_ref[bb] = (hn * _sigmoid(om_ref[bb])).astype(BF16)


def _mlstm_calls(qkvm, kmt, gm, gmt, om, bias_row, bias_col, norm_g, n_tok, n_ctx):
    b, tc, _ = qkvm.shape
    nt = tc // ML_CHUNK
    n_lat, nct = n_tok // ML_CHUNK, n_ctx // ML_CHUNK
    nb = ML_BATCH
    state = [
        pltpu.VMEM((nb, W_ML, W_ML), F32),
        pltpu.VMEM((nb, W_ML, LANES), F32),
        pltpu.VMEM((nb, 8, LANES), F32),
        pltpu.VMEM((nb, 8, LANES), F32),
    ]

    def specs(reverse):
        def tile(g, s):
            return (g, _scan_tile(s, n_lat, nct, reverse), 0)

        def tile_t(g, s):
            return (g, 0, _scan_tile(s, n_lat, nct, reverse))

        row = lambda w: pl.BlockSpec((nb, ML_CHUNK, w), tile)
        common = [row(3 * W_ML), pl.BlockSpec((nb, W_ML, ML_CHUNK), tile_t), row(LANES),
                  pl.BlockSpec((nb, N_GATES, ML_CHUNK), tile_t), _resident((1, LANES)), _resident((N_GATES, LANES))]
        return common, row

    common, row = specs(False)
    hf = pl.pallas_call(
        _mlstm_fwd_kernel,
        grid=(b // nb, nt),
        in_specs=common,
        out_specs=row(W_ML),
        out_shape=jax.ShapeDtypeStruct((b, tc, W_ML), F32),
        scratch_shapes=state,
        compiler_params=_params(("parallel", "arbitrary"), V7X_VMEM_LIMIT_BYTES),
        name="mlstm_forward",
    )(qkvm, kmt, gm, gmt, bias_row, bias_col)
    common, row = specs(True)
    return pl.pallas_call(
        _mlstm_bwd_kernel,
        grid=(b // nb, nt),
        in_specs=common + [row(W_ML), row(W_ML), _resident((1, W_ML))],
        out_specs=row(W_ML),
        out_shape=jax.ShapeDtypeStruct((b, tc, W_ML), BF16),
        scratch_shapes=state,
        compiler_params=_params(("parallel", "arbitrary"), V7X_VMEM_LIMIT_BYTES),
        name="mlstm_backward",
    )(qkvm, kmt, gm, gmt, bias_row, bias_col, hf, om, norm_g.reshape(1, W_ML))


def _merge_kernel(ya_ref, ys_ref, ym_ref, gb_ref, x_ref, mod_ref, g_ref, wa_ref, ws_ref, wm_ref, wo_ref,
                  x1_ref, h2_ref, *, d_model):
    d = d_model
    mod = mod_ref[0]
    y = (_sigmoid(gb_ref[0, :, 0:d].astype(F32)) * _dot(ya_ref[0], wa_ref[...])
         + _sigmoid(gb_ref[0, :, d:2 * d].astype(F32)) * _dot(ys_ref[0], ws_ref[...])
         + _sigmoid(gb_ref[0, :, 2 * d:3 * d].astype(F32)) * _dot(ym_ref[0], wm_ref[...]))
    x1 = x_ref[0] + mod[:, 2 * d:3 * d] * _dot(y.astype(BF16), wo_ref[...])
    x1_ref[0] = x1
    h2_ref[0] = (_rms(x1) * g_ref[...] * (1.0 + mod[:, 4 * d:5 * d]) + mod[:, 3 * d:4 * d]).astype(BF16)


def _merge_call(ya, ys, ym, gb, x, modsel, mod_off, norm_g, wa, ws, wm, wo, tile, blk_off):
    b, n, d = x.shape

    def own(bi, i):
        return (bi, i, 0)

    def shared(bi, i):
        return (bi, i + blk_off, 0)

    return pl.pallas_call(
        functools.partial(_merge_kernel, d_model=d),
        grid=(b, n // tile),
        in_specs=[
            pl.BlockSpec((1, tile, W_ATTN), own),
            pl.BlockSpec((1, tile, W_S5), shared),
            pl.BlockSpec((1, tile, W_ML), shared),
            pl.BlockSpec((1, tile, N_BRANCH * d), shared),
            pl.BlockSpec((1, tile, d), own),
            pl.BlockSpec((1, 1, 6 * d), lambda bi, i: (mod_off + bi, 0, 0)),
            _resident((1, d)),
            _resident((W_ATTN, d)),
            _resident((W_S5, d)),
            _resident((W_ML, d)),
            _resident((d, d)),
        ],
        out_specs=[pl.BlockSpec((1, tile, d), own), pl.BlockSpec((1, tile, d), own)],
        out_shape=(jax.ShapeDtypeStruct((b, n, d), F32), jax.ShapeDtypeStruct((b, n, d), BF16)),
        compiler_params=_params(("parallel", "parallel"), V7X_VMEM_LIMIT_BYTES),
        name="merge",
    )(ya, ys, ym, gb, x, modsel, norm_g.reshape(1, d), wa, ws, wm, wo)


def _ffn_kernel(hp_ref, hm_ref, hn_ref, x1_ref, mod_ref, wup_ref, cw_ref, cb_ref, wdn_ref, fg_ref, o_ref,
                lhs_scr, act_scr, *, d_model, d_ff, tile, final):
    d = d_model
    halo = BF16_SUBLANES
    rows = tile + 2 * halo
    i = pl.program_id(1)
    prev_ok = i >= 1
    next_ok = i < pl.num_programs(1) - 1
    lhs_scr[0:halo] = jnp.where(prev_ok, hp_ref[0], jnp.zeros_like(hp_ref[0]))
    lhs_scr[halo:halo + tile] = hm_ref[0]
    lhs_scr[halo + tile:rows] = jnp.where(next_ok, hn_ref[0], jnp.zeros_like(hn_ref[0]))
    lhs = lhs_scr[...]
    for j in range(d_ff // FFN_CHUNK):
        parts = []
        for lo in (j * FFN_CHUNK, d_ff + j * FFN_CHUNK):
            u = _dot(lhs, wup_ref[:, lo:lo + FFN_CHUNK])
            cw = cw_ref[:, lo:lo + FFN_CHUNK]
            parts.append(pltpu.roll(u, 1, 0)[halo:halo + tile] * cw[0:1]
                         + u[halo:halo + tile] * cw[1:2]
                         + pltpu.roll(u, rows - 1, 0)[halo:halo + tile] * cw[2:3]
                         + cb_ref[:, lo:lo + FFN_CHUNK])
        a, v = parts
        act_scr[:, j * FFN_CHUNK:(j + 1) * FFN_CHUNK] = (a * _sigmoid(a) * v).astype(BF16)
    out = x1_ref[0] + mod_ref[0][:, 5 * d:6 * d] * _dot(act_scr[...], wdn_ref[...])
    if final:
        out = _rms(out) * fg_ref[...]
    o_ref[0] = out


def _ffn_call(h2, x1, modsel, mod_off, w_up, conv_w, conv_b, w_down, final_g, tile, *, final):
    b, n, d = x1.shape
    d_ff = w_down.shape[0]
    halo = BF16_SUBLANES
    per = tile // halo
    n_halo = n // halo
    return pl.pallas_call(
        functools.partial(_ffn_kernel, d_model=d, d_ff=d_ff, tile=tile, final=final),
        grid=(b, n // tile),
        in_specs=[
            pl.BlockSpec((1, halo, d), lambda bi, i: (bi, jnp.maximum(i * per - 1, 0), 0)),
            pl.BlockSpec((1, tile, d), lambda bi, i: (bi, i, 0)),
            pl.BlockSpec((1, halo, d), lambda bi, i: (bi, jnp.minimum((i + 1) * per, n_halo - 1), 0)),
            pl.BlockSpec((1, tile, d), lambda bi, i: (bi, i, 0)),
            pl.BlockSpec((1, 1, 6 * d), lambda bi, i: (mod_off + bi, 0, 0)),
            _resident((d, 2 * d_ff)),
            _resident((3, 2 * d_ff)),
            _resident((1, 2 * d_ff)),
            _resident((d_ff, d)),
            _resident((1, d)),
        ],
        out_specs=pl.BlockSpec((1, tile, d), lambda bi, i: (bi, i, 0)),
        out_shape=jax.ShapeDtypeStruct((b, n, d), F32),
        scratch_shapes=[pltpu.VMEM((tile + 2 * halo, d), BF16), pltpu.VMEM((tile, d_ff), BF16)],
        compiler_params=_params(("parallel", "parallel"), V7X_VMEM_LIMIT_BYTES),
        name="conv_ffn",
    )(h2, h2, h2, x1, modsel, w_up, conv_w, conv_b.reshape(1, 2 * d_ff), w_down, final_g.reshape(1, d))


def _prep_w_in(w):
    d = w.shape[0]
    k0, k1 = w[:, W_ATTN:W_ATTN + HEAD_DIM], w[:, W_ATTN + HEAD_DIM:W_ATTN + 2 * HEAD_DIM]
    o = W_ATTN + 2 * HEAD_DIM
    v0, v1 = w[:, o:o + HEAD_DIM], w[:, o + HEAD_DIM:o + 2 * HEAD_DIM]
    o += 2 * HEAD_DIM
    rest = w[:, o:o + W_S5 + 4 * W_ML]
    o += W_S5 + 4 * W_ML
    gm = w[:, o:o + N_GATES]
    gb = w[:, o + N_GATES:o + N_GATES + N_BRANCH * d]
    km = rest[:, W_S5 + W_ML:W_S5 + 2 * W_ML]
    cols = [w[:, 0:W_ATTN], k0, k0, k1, k1, v0, v0, v1, v1, rest, gb, jnp.tile(gm, (1, LANES // N_GATES))]
    w_t = jnp.concatenate([gm, km], axis=1).T
    return jnp.concatenate(cols, axis=1).astype(BF16), w_t.astype(BF16)


def _rope_tables(n_tok):
    pos = jnp.arange(n_tok)
    row = (pos // GRID_W).astype(F32)
    col = (pos % GRID_W).astype(F32)
    n_freq = HEAD_DIM // 4
    inv_freq = ROPE_BASE ** (-jnp.arange(n_freq, dtype=F32) / n_freq)
    lane = jnp.arange(LANES)
    dd = lane % HEAD_DIM
    by_col = (dd // (HEAD_DIM // 2)) == 1
    d32 = dd % (HEAD_DIM // 2)
    first = d32 < n_freq
    freq = inv_freq[d32 % n_freq]
    ang = jnp.where(by_col[None, :], col[:, None], row[:, None]) * freq[None, :]
    cos, sin = jnp.cos(ang), jnp.sin(ang)
    return cos, jnp.where(first[None, :], -sin, 0.0), jnp.where(first[None, :], 0.0, sin)


def kernel(x, c, ctx, c_ctx, mod_w, mod_b, norm1_g, norm2_g, w_in, attn_sink, s5_a_re, s5_a_im, s5_log_dt, s5_b_re, s5_b_im, s5_c_re, s5_c_im, s5_d, s5_glu_w, s5_glu_b, ml_igate_b, ml_fgate_b, ml_norm_g, w_branch_attn, w_branch_s5, w_branch_ml, w_out, ffn_w_up, ffn_conv_w, ffn_conv_b, ffn_w_down, final_norm_g):
    b, n_tok, d = x.shape
    n_ctx = ctx.shape[1]
    n_layer = mod_w.shape[0]
    assert n_tok % ROW_TILE == 0 and n_tok % FFN_TILE == 0 and n_tok % n_ctx == 0 and n_tok % GRID_W == 0
    assert n_ctx % ATTN_TILE == 0 and n_ctx % ML_CHUNK == 0 and n_ctx % S5_TILE == 0
    assert b % 8 == 0 and b % ML_BATCH == 0 and b + 1 <= 16 and d % LANES == 0
    tc = n_tok + n_ctx

    c_all = jnp.zeros((16, d), F32).at[:b].set(c).at[b].set(c_ctx)
    mod = _mod_call(c_all, mod_w.astype(BF16), mod_b)
    rope = _rope_tables(n_tok)
    xl, xk = x, ctx

    for l in range(n_layer):
        final = l == n_layer - 1
        modsel = jnp.concatenate([jnp.broadcast_to(mod[l, b:b + 1], (b, 6 * d)), mod[l, :b]]).reshape(2 * b, 1, 6 * d)
        w_in_l, w_t = _prep_w_in(w_in[l])
        outs = _in_call(xl, modsel, b, norm1_g[l], w_in_l, w_t, tc, ROW_TILE, 0, rope=rope)
        outs = _in_call(xk, modsel, 0, norm1_g[l], w_in_l, w_t, tc, n_ctx, n_tok // n_ctx, prev=outs)
        qa, kv, us, qkvm, om, gb, gm, gmt, kmt = outs

        ya, ya_ctx = _attn_calls(attn_sink[l], qa, kv, n_tok, n_ctx, with_ctx=not final)

        s5p = _s5_prep(s5_a_re[l], s5_a_im[l], s5_log_dt[l], s5_b_re[l], s5_b_im[l], s5_c_re[l], s5_c_im[l],
                       s5_d[l], s5_glu_w[l], s5_glu_b[l])
        u_rows = us.transpose(1, 0, 2).reshape(tc * b, W_S5)
        ys = _s5_calls(u_rows, s5p, b, n_tok, n_ctx).reshape(tc, b, W_S5).transpose(1, 0, 2)

        gate_b = jnp.concatenate([ml_igate_b[l, 0], ml_fgate_b[l, 0], ml_igate_b[l, 1], ml_fgate_b[l, 1]])
        bias_row = jnp.tile(gate_b, LANES // N_GATES).reshape(1, LANES)
        bias_col = jnp.broadcast_to(gate_b[:, None], (N_GATES, LANES))
        ym = _mlstm_calls(qkvm, kmt, gm, gmt, om, bias_row, bias_col, ml_norm_g[l], n_tok, n_ctx)

        wb = [w.astype(BF16) for w in (w_branch_attn[l], w_branch_s5[l], w_branch_ml[l], w_out[l])]
        wf = (ffn_w_up[l].astype(BF16), ffn_conv_w[l], ffn_conv_b[l], ffn_w_down[l].astype(BF16), final_norm_g)
        x1, h2 = _merge_call(ya, ys, ym, gb, xl, modsel, b, norm2_g[l], *wb, ROW_TILE, 0)
        xl = _ffn_call(h2, x1, modsel, b, *wf, FFN_TILE, final=final)
        if not final:
            x1, h2 = _merge_call(ya_ctx, ys, ym, gb, xk, modsel, 0, norm2_g[l], *wb, n_ctx, n_tok // n_ctx)
            xk = _ffn_call(h2, x1, modsel, 0, *wf, n_ctx, final=False)
    return xl
```

```python
import functools
import math

import jax
import jax.numpy as jnp
from jax import lax
from jax.experimental import pallas as pl
from jax.experimental.pallas import tpu as pltpu

F32 = jnp.float32
BF16 = jnp.bfloat16

N_HEADS = 8
N_KV_HEADS = 2
HEAD_DIM = 64
Q_PER_KV = N_HEADS // N_KV_HEADS
W_ATTN = N_HEADS * HEAD_DIM
WINDOW = 128
GRID_W = 64
ROPE_BASE = 10000.0
W_S5 = 256
S5_GROUP = 16
S5_GROUPS = W_S5 // S5_GROUP
S5_STATE = 64
S5_MODES = S5_GROUPS * S5_STATE
ML_HEADS = 4
ML_HEAD_DIM = 64
W_ML = ML_HEADS * ML_HEAD_DIM
N_GATES = 4 * ML_HEADS
N_BRANCH = 3
NORM_EPS = 1e-6

LANES = 128
BF16_SUBLANES = 16
V7X_VMEM_LIMIT_BYTES = 56 * 1024 * 1024

ROW_TILE = 1024
FFN_TILE = 1024
ATTN_TILE = 256
ML_CHUNK = 256
ML_BATCH = 4
S5_TILE = 128
FFN_CHUNK = 256
NEG_BIG = -1e30
LOG2E = math.log2(math.e)
Q_SCALE = LOG2E * HEAD_DIM ** -0.5

Z_QA = 0
Z_KV = Z_QA + W_ATTN
Z_US = Z_KV + 4 * LANES
Z_QKVM = Z_US + W_S5
Z_OM = Z_QKVM + 3 * W_ML
Z_GB = Z_OM + W_ML


def _sigmoid(x):
    return 1.0 / (1.0 + jnp.exp(-x))


def _log_sigmoid(x):
    return jnp.minimum(x, 0.0) - jnp.log(1.0 + jnp.exp(-jnp.abs(x)))


def _gelu_tanh(x):
    return 0.5 * x * (1.0 + jnp.tanh(math.sqrt(2.0 / math.pi) * (x + 0.044715 * (x * x * x))))


def _rms(x):
    return x * lax.rsqrt(jnp.mean(x * x, axis=-1, keepdims=True) + NORM_EPS)


def _dot(a, b):
    return jnp.dot(a, b, preferred_element_type=F32)


def _dot_nt(a, b):
    return lax.dot_general(a, b, (((1,), (1,)), ((), ())), preferred_element_type=F32)


def _params(sem, vmem=None):
    return pltpu.CompilerParams(dimension_semantics=sem, vmem_limit_bytes=vmem)


def _resident(shape):
    return pl.BlockSpec(shape, lambda *_: (0,) * len(shape), pipeline_mode=pl.Buffered(1))


def _mod_kernel(c_ref, w_ref, b_ref, o_ref):
    c = c_ref[...]
    s = (c * _sigmoid(c)).astype(BF16)
    o_ref[0] = _dot(s, w_ref[0]) + b_ref[0]


def _mod_call(c_all, mod_w, mod_b):
    n_layer, d, d6 = mod_w.shape
    tn = d6 // 4
    rows = c_all.shape[0]
    return pl.pallas_call(
        _mod_kernel,
        grid=(n_layer, d6 // tn),
        in_specs=[
            pl.BlockSpec((rows, d), lambda l, j: (0, 0)),
            pl.BlockSpec((1, d, tn), lambda l, j: (l, 0, j)),
            pl.BlockSpec((1, 1, tn), lambda l, j: (l, 0, j)),
        ],
        out_specs=pl.BlockSpec((1, rows, tn), lambda l, j: (l, 0, j)),
        out_shape=jax.ShapeDtypeStruct((n_layer, rows, d6), F32),
        compiler_params=_params(("parallel", "parallel")),
        name="modulation",
    )(c_all, mod_w, mod_b.reshape(n_layer, 1, d6))


def _rope(xb, cos, s1, s2):
    return xb * cos + pltpu.roll(xb, LANES - 16, 1) * s1 + pltpu.roll(xb, 16, 1) * s2


N_IN_OUT = 9


def _in_kernel(x_ref, mod_ref, g_ref, w_ref, wt_ref, *rest, d_model, rotary):
    d = d_model
    if rotary:
        cos, s1, s2 = (r[...] for r in rest[:3])
        rest = rest[3:]
    qa_ref, kv_ref, us_ref, qkvm_ref, om_ref, gb_ref, gm_ref, gmt_ref, kmt_ref = rest[-N_IN_OUT:]
    k_scale = ML_HEAD_DIM ** -0.5
    mod = mod_ref[0]
    h = (_rms(x_ref[0]) * g_ref[...] * (1.0 + mod[:, d:2 * d]) + mod[:, 0:d]).astype(BF16)

    def proj(lo, width):
        return _dot(h, w_ref[:, lo:lo + width])

    def rot(blk):
        return _rope(blk, cos, s1, s2) if rotary else blk

    zq = proj(Z_QA, W_ATTN)
    for p in range(W_ATTN // LANES):
        sl = slice(p * LANES, (p + 1) * LANES)
        qa_ref[0, :, sl] = (rot(zq[:, sl]) * Q_SCALE).astype(BF16)
    zkv = proj(Z_KV, 4 * LANES)
    for p in range(4):
        sl = slice(p * LANES, (p + 1) * LANES)
        kv_ref[0, :, sl] = (rot(zkv[:, sl]) if p < 2 else zkv[:, sl]).astype(BF16)
    us_ref[0] = proj(Z_US, W_S5)
    zm = proj(Z_QKVM, 3 * W_ML)
    qkvm_ref[0, :, 0:W_ML] = zm[:, 0:W_ML].astype(BF16)
    qkvm_ref[0, :, W_ML:2 * W_ML] = (zm[:, W_ML:2 * W_ML] * k_scale).astype(BF16)
    qkvm_ref[0, :, 2 * W_ML:3 * W_ML] = zm[:, 2 * W_ML:3 * W_ML].astype(BF16)
    om_ref[0] = proj(Z_OM, W_ML)
    for p in range(N_BRANCH):
        gb_ref[0, :, p * d:(p + 1) * d] = proj(Z_GB + p * d, d).astype(BF16)
    gm_ref[0] = proj(Z_GB + N_BRANCH * d, LANES)
    zt = _dot_nt(wt_ref[...], h)
    gmt_ref[0] = zt[0:N_GATES]
    kmt_ref[0] = (zt[N_GATES:N_GATES + W_ML] * k_scale).astype(BF16)


def _in_call(x, modsel, mod_off, norm_g, w_in, w_t, tc, tile, blk_off, rope=None, prev=None):
    b, n, d = x.shape
    nz = w_in.shape[1]
    widths = (W_ATTN, 4 * LANES, W_S5, 3 * W_ML, W_ML, N_BRANCH * d, LANES)
    dtypes = (BF16, BF16, F32, BF16, F32, BF16, F32)
    out_shapes = [jax.ShapeDtypeStruct((b, tc, w), t) for w, t in zip(widths, dtypes)]
    out_shapes.append(jax.ShapeDtypeStruct((b, N_GATES, tc), F32))
    out_shapes.append(jax.ShapeDtypeStruct((b, W_ML, tc), BF16))
    out_specs = [pl.BlockSpec((1, tile, w), lambda bi, i: (bi, i + blk_off, 0)) for w in widths]
    out_specs.append(pl.BlockSpec((1, N_GATES, tile), lambda bi, i: (bi, 0, i + blk_off)))
    out_specs.append(pl.BlockSpec((1, W_ML, tile), lambda bi, i: (bi, 0, i + blk_off)))
    in_specs = [
        pl.BlockSpec((1, tile, d), lambda bi, i: (bi, i, 0)),
        pl.BlockSpec((1, 1, 6 * d), lambda bi, i: (mod_off + bi, 0, 0)),
        _resident((1, d)),
        _resident((d, nz)),
        _resident((N_GATES + W_ML, d)),
    ]
    args = [x, modsel, norm_g.reshape(1, d), w_in, w_t]
    if rope is not None:
        in_specs += [pl.BlockSpec((tile, LANES), lambda bi, i: (i, 0))] * 3
        args += list(rope)
    aliases = {}
    if prev is not None:
        aliases = {len(args) + k: k for k in range(N_IN_OUT)}
        in_specs += [pl.BlockSpec(memory_space=pl.ANY)] * N_IN_OUT
        args += list(prev)
    return pl.pallas_call(
        functools.partial(_in_kernel, d_model=d, rotary=rope is not None),
        grid=(b, n // tile),
        in_specs=in_specs,
        out_specs=out_specs,
        out_shape=out_shapes,
        input_output_aliases=aliases,
        compiler_params=_params(("parallel", "parallel"), V7X_VMEM_LIMIT_BYTES),
        name="in_proj",
    )(*args)


def _attn_heads(sink_ref, q_ref, kv_scr, o_ref, bias, n_key):
    lane = lax.broadcasted_iota(jnp.int32, (n_key, LANES), 1)
    n_q = q_ref.shape[1]
    own = lax.broadcasted_iota(jnp.int32, (n_q, LANES), 1) < HEAD_DIM
    first = lax.broadcasted_iota(jnp.int32, (2 * n_q, 1), 0) < n_q
    n_bias = 0 if bias is None else bias.shape[1]
    if n_bias:
        bias = jnp.concatenate([bias, bias], axis=0)
    for kvh in range(N_KV_HEADS):
        kd = kv_scr[0:n_key, kvh * LANES:(kvh + 1) * LANES]
        vd = kv_scr[0:n_key, (N_KV_HEADS + kvh) * LANES:(N_KV_HEADS + kvh + 1) * LANES]
        blocks = (2 * kvh, 2 * kvh + 1)
        qb = jnp.concatenate([q_ref[0, :, p * LANES:(p + 1) * LANES] for p in blocks], axis=0)
        outs = []
        for par in range(2):
            half = (lane >= HEAD_DIM) if par else (lane < HEAD_DIM)
            km = jnp.where(half, kd, jnp.zeros_like(kd))
            vm = jnp.where(half, vd, jnp.ones_like(vd))
            s = _dot_nt(qb, km)
            parts = [s[:, 0:n_bias] + bias, s[:, n_bias:n_key]] if n_bias else [s]
            sk = jnp.where(first, sink_ref[2 * blocks[0] + par], sink_ref[2 * blocks[1] + par]) * LOG2E
            m = sk
            for part in parts:
                m = jnp.maximum(m, jnp.max(part, axis=1, keepdims=True))
            e = jnp.concatenate([jnp.exp2(part - m).astype(BF16) for part in parts], axis=1)
            o = _dot(e, vm)
            den = pltpu.roll(o, HEAD_DIM, 1) + jnp.exp2(sk - m)
            outs.append(o / den)
        for j, p in enumerate(blocks):
            rows = slice(j * n_q, (j + 1) * n_q)
            o_ref[0, :, p * LANES:(p + 1) * LANES] = jnp.where(own, outs[0][rows], outs[1][rows]).astype(BF16)


def _attn_latent_kernel(sink_ref, q_ref, kvp_ref, kvm_ref, kvn_ref, kvc_ref, o_ref, kv_scr, *, n_tok, n_ctx):
    i = pl.program_id(1)
    n_loc = ATTN_TILE + 2 * WINDOW
    n_key = n_loc + n_ctx
    kv_scr[0:WINDOW] = kvp_ref[0]
    kv_scr[WINDOW:WINDOW + ATTN_TILE] = kvm_ref[0]
    kv_scr[WINDOW + ATTN_TILE:n_loc] = kvn_ref[0]
    kv_scr[n_loc:n_key] = kvc_ref[0]
    t0 = i * ATTN_TILE
    row = lax.broadcasted_iota(jnp.int32, (ATTN_TILE, n_loc), 0)
    col = lax.broadcasted_iota(jnp.int32, (ATTN_TILE, n_loc), 1)
    lo = jnp.maximum(row, WINDOW - t0)
    hi = jnp.minimum(row + 2 * WINDOW, n_tok - t0 + WINDOW - 1)
    valid = jnp.logical_and(col >= lo, col <= hi)
    _attn_heads(sink_ref, q_ref, kv_scr, o_ref, jnp.where(valid, 0.0, NEG_BIG), n_key)


def _attn_context_kernel(sink_ref, q_ref, kvc_ref, o_ref, kv_scr, *, n_ctx):
    kv_scr[...] = kvc_ref[0]
    _attn_heads(sink_ref, q_ref, kv_scr, o_ref, None, n_ctx)


def _attn_calls(sink, qa, kv, n_tok, n_ctx, with_ctx):
    b = qa.shape[0]
    kvw = kv.shape[2]
    per = ATTN_TILE // WINDOW
    ctx_blk = n_tok // n_ctx
    last_half = (n_tok + n_ctx) // WINDOW - 1
    smem = pl.BlockSpec(memory_space=pltpu.SMEM)
    ya = pl.pallas_call(
        functools.partial(_attn_latent_kernel, n_tok=n_tok, n_ctx=n_ctx),
        grid=(b, n_tok // ATTN_TILE),
        in_specs=[
            smem,
            pl.BlockSpec((1, ATTN_TILE, W_ATTN), lambda bi, i: (bi, i, 0)),
            pl.BlockSpec((1, WINDOW, kvw), lambda bi, i: (bi, jnp.maximum(per * i - 1, 0), 0)),
            pl.BlockSpec((1, ATTN_TILE, kvw), lambda bi, i: (bi, i, 0)),
            pl.BlockSpec((1, WINDOW, kvw), lambda bi, i: (bi, jnp.minimum(per * i + per, last_half), 0)),
            pl.BlockSpec((1, n_ctx, kvw), lambda bi, i: (bi, ctx_blk, 0)),
        ],
        out_specs=pl.BlockSpec((1, ATTN_TILE, W_ATTN), lambda bi, i: (bi, i, 0)),
        out_shape=jax.ShapeDtypeStruct((b, n_tok, W_ATTN), BF16),
        scratch_shapes=[pltpu.VMEM((ATTN_TILE + 2 * WINDOW + n_ctx, kvw), BF16)],
        compiler_params=_params(("parallel", "parallel"), V7X_VMEM_LIMIT_BYTES),
        name="attention",
    )(sink, qa, kv, kv, kv, kv)
    if not with_ctx:
        return ya, None
    ya_ctx = pl.pallas_call(
        functools.partial(_attn_context_kernel, n_ctx=n_ctx),
        grid=(b,),
        in_specs=[
            smem,
            pl.BlockSpec((1, n_ctx, W_ATTN), lambda bi: (bi, ctx_blk, 0)),
            pl.BlockSpec((1, n_ctx, kvw), lambda bi: (bi, ctx_blk, 0)),
        ],
        out_specs=pl.BlockSpec((1, n_ctx, W_ATTN), lambda bi: (bi, 0, 0)),
        out_shape=jax.ShapeDtypeStruct((b, n_ctx, W_ATTN), BF16),
        scratch_shapes=[pltpu.VMEM((n_ctx, kvw), BF16)],
        compiler_params=_params(("parallel",)),
        name="attention_ctx",
    )(sink, qa, kv)
    return ya, ya_ctx


def _scan_tile(step, n_lat, n_ctx_tiles, reverse):
    if reverse:
        return jnp.where(step < n_ctx_tiles, n_lat + n_ctx_tiles - 1 - step, n_lat - 1 - (step - n_ctx_tiles))
    return jnp.where(step < n_ctx_tiles, n_lat + step, step - n_ctx_tiles)


def _s5_scan_tile(u_ref, bblk_ref, cblk_ref, lam_ref, x_scr, st_scr, *, reverse, n_batch):
    i = pl.program_id(0)

    @pl.when(i == 0)
    def _():
        st_scr[...] = jnp.zeros_like(st_scr)

    u = u_ref[...]
    x_scr[...] = _dot(u.astype(BF16), bblk_ref[...])
    lre = jnp.broadcast_to(lam_ref[0:1, :], (n_batch, S5_MODES))
    lim = jnp.broadcast_to(lam_ref[1:2, :], (n_batch, S5_MODES))

    re, im = st_scr[0], st_scr[1]
    for j in range(S5_TILE):
        r0 = ((S5_TILE - 1 - j) if reverse else j) * n_batch
        bre = x_scr[r0:r0 + n_batch, 0:S5_MODES]
        bim = x_scr[r0:r0 + n_batch, S5_MODES:2 * S5_MODES]
        re, im = lre * re - lim * im + bre, lre * im + lim * re + bim
        x_scr[r0:r0 + n_batch, 0:S5_MODES] = re
        x_scr[r0:r0 + n_batch, S5_MODES:2 * S5_MODES] = im
    st_scr[0] = re
    st_scr[1] = im
    return u, _dot(x_scr[...].astype(BF16), cblk_ref[...])


def _s5_fwd_kernel(u_ref, bblk_ref, cblk_ref, lam_ref, o_ref, x_scr, st_scr, *, n_batch):
    _, y = _s5_scan_tile(u_ref, bblk_ref, cblk_ref, lam_ref, x_scr, st_scr, reverse=False, n_batch=n_batch)
    o_ref[...] = y


def _s5_bwd_kernel(u_ref, bblk_ref, cblk_ref, lam_ref, yf_ref, dsk_ref, wg_ref, bg_ref, o_ref, x_scr, st_scr,
                   *, n_batch):
    u, y = _s5_scan_tile(u_ref, bblk_ref, cblk_ref, lam_ref, x_scr, st_scr, reverse=True, n_batch=n_batch)
    y = _gelu_tanh(y + yf_ref[...] + dsk_ref[...] * u)
    o_ref[...] = (y * _sigmoid(_dot(y.astype(BF16), wg_ref[...]) + bg_ref[...])).astype(BF16)


def _s5_calls(u_rows, prm, n_batch, n_tok, n_ctx):
    rows_total = u_rows.shape[0]
    tile_rows = S5_TILE * n_batch
    nt = rows_total // tile_rows
    n_lat, nct = n_tok // S5_TILE, n_ctx // S5_TILE

    def fwd_tile(i):
        return (_scan_tile(i, n_lat, nct, False), 0)

    def bwd_tile(i):
        return (_scan_tile(i, n_lat, nct, True), 0)

    scratch = [pltpu.VMEM((tile_rows, 2 * S5_MODES), F32), pltpu.VMEM((2, n_batch, S5_MODES), F32)]
    wspecs = [_resident((W_S5, 2 * S5_MODES)), _resident((2 * S5_MODES, W_S5)), _resident((2, S5_MODES))]
    yf = pl.pallas_call(
        functools.partial(_s5_fwd_kernel, n_batch=n_batch),
        grid=(nt,),
        in_specs=[pl.BlockSpec((tile_rows, W_S5), fwd_tile)] + wspecs,
        out_specs=pl.BlockSpec((tile_rows, W_S5), fwd_tile),
        out_shape=jax.ShapeDtypeStruct((rows_total, W_S5), F32),
        scratch_shapes=scratch,
        compiler_params=_params(("arbitrary",), V7X_VMEM_LIMIT_BYTES),
        name="s5_forward",
    )(u_rows, prm["bblk"][0], prm["cblk"], prm["lam"][0])
    return pl.pallas_call(
        functools.partial(_s5_bwd_kernel, n_batch=n_batch),
        grid=(nt,),
        in_specs=[pl.BlockSpec((tile_rows, W_S5), bwd_tile)] + wspecs + [
            pl.BlockSpec((tile_rows, W_S5), bwd_tile),
            _resident((1, W_S5)),
            _resident((W_S5, W_S5)),
            _resident((1, W_S5)),
        ],
        out_specs=pl.BlockSpec((tile_rows, W_S5), bwd_tile),
        out_shape=jax.ShapeDtypeStruct((rows_total, W_S5), BF16),
        scratch_shapes=scratch,
        compiler_params=_params(("arbitrary",), V7X_VMEM_LIMIT_BYTES),
        name="s5_backward",
    )(u_rows, prm["bblk"][1], prm["cblk"], prm["lam"][1], yf, prm["dsk"], prm["wglu"], prm["bglu"])


def _s5_prep(a_re, a_im, log_dt, b_re, b_im, c_re, c_im, d_skip, glu_w, glu_b):
    eye = jnp.eye(S5_GROUPS, dtype=F32)
    bblk, lam = [], []
    for d in range(2):
        dt = jnp.exp(log_dt[d])[:, None]
        dre, dim = dt * a_re[d], dt * a_im[d]
        mag = jnp.exp(dre)
        lam_re, lam_im = mag * jnp.cos(dim), mag * jnp.sin(dim)
        den = a_re[d] * a_re[d] + a_im[d] * a_im[d]
        coef_re = ((lam_re - 1.0) * a_re[d] + lam_im * a_im[d]) / den
        coef_im = (lam_im * a_re[d] - (lam_re - 1.0) * a_im[d]) / den
        cr, ci = coef_re[..., None], coef_im[..., None]
        bb_re = cr * b_re - ci * b_im
        bb_im = cr * b_im + ci * b_re
        blk = [jnp.einsum("gpc,gh->gchp", bb, eye).reshape(W_S5, S5_MODES) for bb in (bb_re, bb_im)]
        bblk.append(jnp.concatenate(blk, axis=1).astype(BF16))
        lam.append(jnp.stack([lam_re.reshape(-1), lam_im.reshape(-1)]))
    cblk = [jnp.einsum("gcp,gh->gphc", cc, eye).reshape(S5_MODES, W_S5) for cc in (c_re, -c_im)]
    return dict(bblk=bblk, lam=lam, cblk=jnp.concatenate(cblk, axis=0).astype(BF16),
                dsk=d_skip.reshape(1, W_S5), wglu=glu_w.astype(BF16), bglu=glu_b.reshape(1, W_S5))


def _head_blocks(cols):
    shape = (cols[0].shape[0], LANES)
    first = lax.broadcasted_iota(jnp.int32, shape, 1) < ML_HEAD_DIM
    return jnp.concatenate([jnp.where(first, cols[0], cols[1]), jnp.where(first, cols[2], cols[3])], axis=1)


def _split3(x):
    hi = x.astype(BF16)
    rem = x - hi.astype(F32)
    mid = rem.astype(BF16)
    return hi, mid, (rem - mid.astype(F32)).astype(BF16)


def _select_cols(x, sel, terms=3):
    return sum(_dot(part, sel) for part in _split3(x)[:terms])


def _mlstm_chunk(bb, qkv_ref, kmt_ref, gm_ref, gmt_ref, brow_ref, bcol_ref, ct_scr, n_scr, mc_scr, mr_scr,
                 *, reverse):
    n = ML_CHUNK
    g0 = 2 * ML_HEADS * (1 if reverse else 0)
    last = 0 if reverse else n - 1
    q = qkv_ref[bb, :, 0:W_ML]
    k = qkv_ref[bb, :, W_ML:2 * W_ML]
    v = qkv_ref[bb, :, 2 * W_ML:3 * W_ML]
    ti = lax.broadcasted_iota(jnp.int32, (n, n), 0)
    si = lax.broadcasted_iota(jnp.int32, (n, n), 1)
    causal = (si >= ti) if reverse else (si <= ti)
    tri = jnp.where(causal, 1.0, 0.0).astype(BF16)
    tri_t = jnp.where((ti >= si) if reverse else (ti <= si), 1.0, 0.0).astype(BF16)

    g_col = gm_ref[bb] + brow_ref[...]
    lf_col = _log_sigmoid(pltpu.roll(g_col, LANES - ML_HEADS, 1))
    bcum = sum(_dot(tri, part) for part in _split3(lf_col))
    r_col = g_col - bcum
    rowi = lax.broadcasted_iota(jnp.int32, (n, LANES), 0)
    cmax = r_col
    sh = 1
    while sh < n:
        if reverse:
            cmax = jnp.where(rowi < n - sh, jnp.maximum(cmax, pltpu.roll(cmax, n - sh, 0)), cmax)
        else:
            cmax = jnp.where(rowi >= sh, jnp.maximum(cmax, pltpu.roll(cmax, sh, 0)), cmax)
        sh *= 2
    m_col = mc_scr[bb, 0:1, :]
    mm = jnp.maximum(m_col, cmax)
    mm_last = mm[last:last + 1, :]
    mc_scr[bb, 0:1, :] = bcum[last:last + 1, :] + mm_last
    used = jnp.logical_and(lax.broadcasted_iota(jnp.int32, (n, LANES), 1) >= g0,
                           lax.broadcasted_iota(jnp.int32, (n, LANES), 1) < g0 + ML_HEADS)

    src = lax.broadcasted_iota(jnp.int32, (LANES, W_ML), 0)
    dst = lax.broadcasted_iota(jnp.int32, (LANES, W_ML), 1) // ML_HEAD_DIM
    sel_head = jnp.where(src == g0 + dst, 1.0, 0.0).astype(BF16)
    wi_b = _select_cols(jnp.where(used, jnp.exp(m_col - mm), 0.0), sel_head, terms=2)
    floor_b = jnp.exp(_select_cols(jnp.where(used, -(bcum + mm), 0.0), sel_head))

    g_row = gmt_ref[bb, g0:g0 + 2 * ML_HEADS, :] + bcol_ref[g0:g0 + 2 * ML_HEADS, 0:1]
    lf_row = _log_sigmoid(g_row)
    bcum_row = sum(_dot(part, tri_t) for part in _split3(lf_row))
    r_row = g_row[0:ML_HEADS] - bcum_row[ML_HEADS:2 * ML_HEADS]
    m_row = mr_scr[bb, 0:ML_HEADS, :]
    mm_row = jnp.maximum(m_row, jnp.max(r_row, axis=1, keepdims=True))
    decay_row = jnp.exp(m_row - mm_row)
    mr_scr[bb, 0:ML_HEADS, :] = jnp.sum(lf_row[ML_HEADS:2 * ML_HEADS], axis=1, keepdims=True) + mm_row
    w_state = jnp.exp(r_row - mm_row[:, 0:1])
    decay_col = jnp.concatenate(
        [jnp.broadcast_to(decay_row[h:h + 1, :], (ML_HEAD_DIM, LANES)) for h in range(ML_HEADS)], axis=0)
    w_state_rows = jnp.concatenate(
        [jnp.broadcast_to(w_state[h:h + 1, :], (ML_HEAD_DIM, n)) for h in range(ML_HEADS)], axis=0)

    lane_head = lax.broadcasted_iota(jnp.int32, (n, W_ML), 1) // ML_HEAD_DIM
    head_mask = [jnp.where(lane_head == h, 1.0, 0.0).astype(BF16) for h in range(ML_HEADS)]
    kexp = jnp.concatenate([k * mk for mk in head_mask], axis=0)
    s_all = _dot_nt(q, kexp)
    own = lax.broadcasted_iota(jnp.int32, (n, LANES), 1) < ML_HEAD_DIM
    own_b = jnp.where(own, 1.0, 0.0).astype(BF16)
    other_b = jnp.where(own, 0.0, 1.0).astype(BF16)
    per_head = []
    for h in range(ML_HEADS):
        mm_b = jnp.broadcast_to(mm[:, g0 + h:g0 + h + 1], (n, n))
        arg = jnp.where(causal, r_row[h:h + 1, :] - mm_b, NEG_BIG)
        w = (jnp.exp(arg) * s_all[:, h * n:(h + 1) * n]).astype(BF16)
        v_pair = v[:, (h // 2) * LANES:(h // 2 + 1) * LANES]
        keep, fill = (own_b, other_b) if h % 2 == 0 else (other_b, own_b)
        per_head.append(_dot(w, v_pair * keep + fill))
    pairs = [(per_head[2 * p], per_head[2 * p + 1]) for p in range(ML_HEADS // 2)]
    num_intra = jnp.concatenate([jnp.where(own, even, odd) for even, odd in pairs], axis=1)
    sum_intra = jnp.concatenate(
        [pltpu.roll(jnp.where(own, odd, even), ML_HEAD_DIM, 1) for even, odd in pairs], axis=1)

    ct = ct_scr[bb]
    nst = n_scr[bb]
    bi = lax.broadcasted_iota(jnp.int32, (W_ML, W_ML), 0) // ML_HEAD_DIM
    bj = lax.broadcasted_iota(jnp.int32, (W_ML, W_ML), 1) // ML_HEAD_DIM
    same_head = bi == bj
    nblk = jnp.where(same_head, jnp.concatenate([nst, nst], axis=1), 0.0)
    num = wi_b * _dot(q, ct.astype(BF16)) + num_intra
    den = wi_b * _dot(q, nblk.astype(BF16)) + sum_intra
    h_out = num / jnp.maximum(jnp.abs(den), floor_b)

    ktw = (kmt_ref[bb].astype(F32) * w_state_rows).astype(BF16)
    vext = jnp.concatenate([v, jnp.ones((n, LANES), BF16)], axis=1)
    upd = _dot(ktw, vext)
    ct_scr[bb] = jnp.concatenate([decay_col, decay_col], axis=1) * ct + jnp.where(same_head, upd[:, 0:W_ML], 0.0)
    n_scr[bb] = decay_col * nst + upd[:, W_ML:W_ML + LANES]
    return h_out


def _mlstm_reset(ct_scr, n_scr, mc_scr, mr_scr):
    @pl.when(pl.program_id(1) == 0)
    def _():
        ct_scr[...] = jnp.zeros_like(ct_scr)
        n_scr[...] = jnp.zeros_like(n_scr)
        mc_scr[...] = jnp.zeros_like(mc_scr)
        mr_scr[...] = jnp.zeros_like(mr_scr)


def _mlstm_fwd_kernel(qkv_ref, kmt_ref, gm_ref, gmt_ref, brow_ref, bcol_ref, hf_ref, *state):
    _mlstm_reset(*state)
    for bb in range(ML_BATCH):
        hf_ref[bb] = _mlstm_chunk(bb, qkv_ref, kmt_ref, gm_ref, gmt_ref, brow_ref, bcol_ref, *state, reverse=False)


def _mlstm_bwd_kernel(qkv_ref, kmt_ref, gm_ref, gmt_ref, brow_ref, bcol_ref, hf_ref, om_ref, ng_ref, o_ref, *state):
    _mlstm_reset(*state)
    lane_head = lax.broadcasted_iota(jnp.int32, (ML_CHUNK, W_ML), 1) // ML_HEAD_DIM
    for bb in range(ML_BATCH):
        ht = hf_ref[bb] + _mlstm_chunk(bb, qkv_ref, kmt_ref, gm_ref, gmt_ref, brow_ref, bcol_ref, *state,
                                       reverse=True)
        sq = ht * ht
        ms = [jnp.sum(jnp.where(lane_head == h, sq, 0.0), axis=1, keepdims=True) * (1.0 / ML_HEAD_DIM)
              for h in range(ML_HEADS)]
        hn = ht * lax.rsqrt(_head_blocks(ms) + NORM_EPS) * ng_ref[...]
        o_ref[bb] = (hn * _sigmoid(om_ref[bb])).astype(BF16)


def _mlstm_calls(qkvm, kmt, gm, gmt, om, bias_row, bias_col, norm_g, n_tok, n_ctx):
    b, tc, _ = qkvm.shape
    nt = tc // ML_CHUNK
    n_lat, nct = n_tok // ML_CHUNK, n_ctx // ML_CHUNK
    nb = ML_BATCH
    state = [
        pltpu.VMEM((nb, W_ML, W_ML), F32),
        pltpu.VMEM((nb, W_ML, LANES), F32),
        pltpu.VMEM((nb, 8, LANES), F32),
        pltpu.VMEM((nb, 8, LANES), F32),
    ]

    def specs(reverse):
        def tile(g, s):
            return (g, _scan_tile(s, n_lat, nct, reverse), 0)

        def tile_t(g, s):
            return (g, 0, _scan_tile(s, n_lat, nct, reverse))

        row = lambda w: pl.BlockSpec((nb, ML_CHUNK, w), tile)
        common = [row(3 * W_ML), pl.BlockSpec((nb, W_ML, ML_CHUNK), tile_t), row(LANES),
                  pl.BlockSpec((nb, N_GATES, ML_CHUNK), tile_t), _resident((1, LANES)), _resident((N_GATES, LANES))]
        return common, row

    common, row = specs(False)
    hf = pl.pallas_call(
        _mlstm_fwd_kernel,
        grid=(b // nb, nt),
        in_specs=common,
        out_specs=row(W_ML),
        out_shape=jax.ShapeDtypeStruct((b, tc, W_ML), F32),
        scratch_shapes=state,
        compiler_params=_params(("parallel", "arbitrary"), V7X_VMEM_LIMIT_BYTES),
        name="mlstm_forward",
    )(qkvm, kmt, gm, gmt, bias_row, bias_col)
    common, row = specs(True)
    return pl.pallas_call(
        _mlstm_bwd_kernel,
        grid=(b // nb, nt),
        in_specs=common + [row(W_ML), row(W_ML), _resident((1, W_ML))],
        out_specs=row(W_ML),
        out_shape=jax.ShapeDtypeStruct((b, tc, W_ML), BF16),
        scratch_shapes=state,
        compiler_params=_params(("parallel", "arbitrary"), V7X_VMEM_LIMIT_BYTES),
        name="mlstm_backward",
    )(qkvm, kmt, gm, gmt, bias_row, bias_col, hf, om, norm_g.reshape(1, W_ML))


def _merge_kernel(ya_ref, ys_ref, ym_ref, gb_ref, x_ref, mod_ref, g_ref, wa_ref, ws_ref, wm_ref, wo_ref,
                  x1_ref, h2_ref, *, d_model):
    d = d_model
    mod = mod_ref[0]
    y = (_sigmoid(gb_ref[0, :, 0:d].astype(F32)) * _dot(ya_ref[0], wa_ref[...])
         + _sigmoid(gb_ref[0, :, d:2 * d].astype(F32)) * _dot(ys_ref[0], ws_ref[...])
         + _sigmoid(gb_ref[0, :, 2 * d:3 * d].astype(F32)) * _dot(ym_ref[0], wm_ref[...]))
    x1 = x_ref[0] + mod[:, 2 * d:3 * d] * _dot(y.astype(BF16), wo_ref[...])
    x1_ref[0] = x1
    h2_ref[0] = (_rms(x1) * g_ref[...] * (1.0 + mod[:, 4 * d:5 * d]) + mod[:, 3 * d:4 * d]).astype(BF16)


def _merge_call(ya, ys, ym, gb, x, modsel, mod_off, norm_g, wa, ws, wm, wo, tile, blk_off):
    b, n, d = x.shape

    def own(bi, i):
        return (bi, i, 0)

    def shared(bi, i):
        return (bi, i + blk_off, 0)

    return pl.pallas_call(
        functools.partial(_merge_kernel, d_model=d),
        grid=(b, n // tile),
        in_specs=[
            pl.BlockSpec((1, tile, W_ATTN), own),
            pl.BlockSpec((1, tile, W_S5), shared),
            pl.BlockSpec((1, tile, W_ML), shared),
            pl.BlockSpec((1, tile, N_BRANCH * d), shared),
            pl.BlockSpec((1, tile, d), own),
            pl.BlockSpec((1, 1, 6 * d), lambda bi, i: (mod_off + bi, 0, 0)),
            _resident((1, d)),
            _resident((W_ATTN, d)),
            _resident((W_S5, d)),
            _resident((W_ML, d)),
            _resident((d, d)),
        ],
        out_specs=[pl.BlockSpec((1, tile, d), own), pl.BlockSpec((1, tile, d), own)],
        out_shape=(jax.ShapeDtypeStruct((b, n, d), F32), jax.ShapeDtypeStruct((b, n, d), BF16)),
        compiler_params=_params(("parallel", "parallel"), V7X_VMEM_LIMIT_BYTES),
        name="merge",
    )(ya, ys, ym, gb, x, modsel, norm_g.reshape(1, d), wa, ws, wm, wo)


def _ffn_kernel(hp_ref, hm_ref, hn_ref, x1_ref, mod_ref, wup_ref, cw_ref, cb_ref, wdn_ref, fg_ref, o_ref,
                lhs_scr, act_scr, *, d_model, d_ff, tile, final):
    d = d_model
    halo = BF16_SUBLANES
    rows = tile + 2 * halo
    i = pl.program_id(1)
    prev_ok = i >= 1
    next_ok = i < pl.num_programs(1) - 1
    lhs_scr[0:halo] = jnp.where(prev_ok, hp_ref[0], jnp.zeros_like(hp_ref[0]))
    lhs_scr[halo:halo + tile] = hm_ref[0]
    lhs_scr[halo + tile:rows] = jnp.where(next_ok, hn_ref[0], jnp.zeros_like(hn_ref[0]))
    lhs = lhs_scr[...]
    for j in range(d_ff // FFN_CHUNK):
        parts = []
        for lo in (j * FFN_CHUNK, d_ff + j * FFN_CHUNK):
            u = _dot(lhs, wup_ref[:, lo:lo + FFN_CHUNK])
            cw = cw_ref[:, lo:lo + FFN_CHUNK]
            parts.append(pltpu.roll(u, 1, 0)[halo:halo + tile] * cw[0:1]
                         + u[halo:halo + tile] * cw[1:2]
                         + pltpu.roll(u, rows - 1, 0)[halo:halo + tile] * cw[2:3]
                         + cb_ref[:, lo:lo + FFN_CHUNK])
        a, v = parts
        act_scr[:, j * FFN_CHUNK:(j + 1) * FFN_CHUNK] = (a * _sigmoid(a) * v).astype(BF16)
    out = x1_ref[0] + mod_ref[0][:, 5 * d:6 * d] * _dot(act_scr[...], wdn_ref[...])
    if final:
        out = _rms(out) * fg_ref[...]
    o_ref[0] = out


def _ffn_call(h2, x1, modsel, mod_off, w_up, conv_w, conv_b, w_down, final_g, tile, *, final):
    b, n, d = x1.shape
    d_ff = w_down.shape[0]
    halo = BF16_SUBLANES
    per = tile // halo
    n_halo = n // halo
    return pl.pallas_call(
        functools.partial(_ffn_kernel, d_model=d, d_ff=d_ff, tile=tile, final=final),
        grid=(b, n // tile),
        in_specs=[
            pl.BlockSpec((1, halo, d), lambda bi, i: (bi, jnp.maximum(i * per - 1, 0), 0)),
            pl.BlockSpec((1, tile, d), lambda bi, i: (bi, i, 0)),
            pl.BlockSpec((1, halo, d), lambda bi, i: (bi, jnp.minimum((i + 1) * per, n_halo - 1), 0)),
            pl.BlockSpec((1, tile, d), lambda bi, i: (bi, i, 0)),
            pl.BlockSpec((1, 1, 6 * d), lambda bi, i: (mod_off + bi, 0, 0)),
            _resident((d, 2 * d_ff)),
            _resident((3, 2 * d_ff)),
            _resident((1, 2 * d_ff)),
            _resident((d_ff, d)),
            _resident((1, d)),
        ],
        out_specs=pl.BlockSpec((1, tile, d), lambda bi, i: (bi, i, 0)),
        out_shape=jax.ShapeDtypeStruct((b, n, d), F32),
        scratch_shapes=[pltpu.VMEM((tile + 2 * halo, d), BF16), pltpu.VMEM((tile, d_ff), BF16)],
        compiler_params=_params(("parallel", "parallel"), V7X_VMEM_LIMIT_BYTES),
        name="conv_ffn",
    )(h2, h2, h2, x1, modsel, w_up, conv_w, conv_b.reshape(1, 2 * d_ff), w_down, final_g.reshape(1, d))


def _prep_w_in(w):
    d = w.shape[0]
    k0, k1 = w[:, W_ATTN:W_ATTN + HEAD_DIM], w[:, W_ATTN + HEAD_DIM:W_ATTN + 2 * HEAD_DIM]
    o = W_ATTN + 2 * HEAD_DIM
    v0, v1 = w[:, o:o + HEAD_DIM], w[:, o + HEAD_DIM:o + 2 * HEAD_DIM]
    o += 2 * HEAD_DIM
    rest = w[:, o:o + W_S5 + 4 * W_ML]
    o += W_S5 + 4 * W_ML
    gm = w[:, o:o + N_GATES]
    gb = w[:, o + N_GATES:o + N_GATES + N_BRANCH * d]
    km = rest[:, W_S5 + W_ML:W_S5 + 2 * W_ML]
    cols = [w[:, 0:W_ATTN], k0, k0, k1, k1, v0, v0, v1, v1, rest, gb, jnp.tile(gm, (1, LANES // N_GATES))]
    w_t = jnp.concatenate([gm, km], axis=1).T
    return jnp.concatenate(cols, axis=1).astype(BF16), w_t.astype(BF16)


def _rope_tables(n_tok):
    pos = jnp.arange(n_tok)
    row = (pos // GRID_W).astype(F32)
    col = (pos % GRID_W).astype(F32)
    n_freq = HEAD_DIM // 4
    inv_freq = ROPE_BASE ** (-jnp.arange(n_freq, dtype=F32) / n_freq)
    lane = jnp.arange(LANES)
    dd = lane % HEAD_DIM
    by_col = (dd // (HEAD_DIM // 2)) == 1
    d32 = dd % (HEAD_DIM // 2)
    first = d32 < n_freq
    freq = inv_freq[d32 % n_freq]
    ang = jnp.where(by_col[None, :], col[:, None], row[:, None]) * freq[None, :]
    cos, sin = jnp.cos(ang), jnp.sin(ang)
    return cos, jnp.where(first[None, :], -sin, 0.0), jnp.where(first[None, :], 0.0, sin)


def kernel(x, c, ctx, c_ctx, mod_w, mod_b, norm1_g, norm2_g, w_in, attn_sink, s5_a_re, s5_a_im, s5_log_dt, s5_b_re, s5_b_im, s5_c_re, s5_c_im, s5_d, s5_glu_w, s5_glu_b, ml_igate_b, ml_fgate_b, ml_norm_g, w_branch_attn, w_branch_s5, w_branch_ml, w_out, ffn_w_up, ffn_conv_w, ffn_conv_b, ffn_w_down, final_norm_g):
    b, n_tok, d = x.shape
    n_ctx = ctx.shape[1]
    n_layer = mod_w.shape[0]
    assert n_tok % ROW_TILE == 0 and n_tok % FFN_TILE == 0 and n_tok % n_ctx == 0 and n_tok % GRID_W == 0
    assert n_ctx % ATTN_TILE == 0 and n_ctx % ML_CHUNK == 0 and n_ctx % S5_TILE == 0
    assert b % 8 == 0 and b % ML_BATCH == 0 and b + 1 <= 16 and d % LANES == 0
    tc = n_tok + n_ctx

    c_all = jnp.zeros((16, d), F32).at[:b].set(c).at[b].set(c_ctx)
    mod = _mod_call(c_all, mod_w.astype(BF16), mod_b)
    rope = _rope_tables(n_tok)
    xl, xk = x, ctx

    for l in range(n_layer):
        final = l == n_layer - 1
        modsel = jnp.concatenate([jnp.broadcast_to(mod[l, b:b + 1], (b, 6 * d)), mod[l, :b]]).reshape(2 * b, 1, 6 * d)
        w_in_l, w_t = _prep_w_in(w_in[l])
        outs = _in_call(xl, modsel, b, norm1_g[l], w_in_l, w_t, tc, ROW_TILE, 0, rope=rope)
        outs = _in_call(xk, modsel, 0, norm1_g[l], w_in_l, w_t, tc, n_ctx, n_tok // n_ctx, prev=outs)
        qa, kv, us, qkvm, om, gb, gm, gmt, kmt = outs

        ya, ya_ctx = _attn_calls(attn_sink[l], qa, kv, n_tok, n_ctx, with_ctx=not final)

        s5p = _s5_prep(s5_a_re[l], s5_a_im[l], s5_log_dt[l], s5_b_re[l], s5_b_im[l], s5_c_re[l], s5_c_im[l],
                       s5_d[l], s5_glu_w[l], s5_glu_b[l])
        u_rows = us.transpose(1, 0, 2).reshape(tc * b, W_S5)
        ys = _s5_calls(u_rows, s5p, b, n_tok, n_ctx).reshape(tc, b, W_S5).transpose(1, 0, 2)

        gate_b = jnp.concatenate([ml_igate_b[l, 0], ml_fgate_b[l, 0], ml_igate_b[l, 1], ml_fgate_b[l, 1]])
        bias_row = jnp.tile(gate_b, LANES // N_GATES).reshape(1, LANES)
        bias_col = jnp.broadcast_to(gate_b[:, None], (N_GATES, LANES))
        ym = _mlstm_calls(qkvm, kmt, gm, gmt, om, bias_row, bias_col, ml_norm_g[l], n_tok, n_ctx)

        wb = [w.astype(BF16) for w in (w_branch_attn[l], w_branch_s5[l], w_branch_ml[l], w_out[l])]
        wf = (ffn_w_up[l].astype(BF16), ffn_conv_w[l], ffn_conv_b[l], ffn_w_down[l].astype(BF16), final_norm_g)
        x1, h2 = _merge_call(ya, ys, ym, gb, xl, modsel, b, norm2_g[l], *wb, ROW_TILE, 0)
        xl = _ffn_call(h2, x1, modsel, b, *wf, FFN_TILE, final=final)
        if not final:
            x1, h2 = _merge_call(ya_ctx, ys, ym, gb, xk, modsel, 0, norm2_g[l], *wb, n_ctx, n_tok // n_ctx)
            xk = _ffn_call(h2, x1, modsel, 0, *wf, n_ctx, final=False)
    return xl
```

```python
import functools
import math

import jax
import jax.numpy as jnp
from jax import lax
from jax.experimental import pallas as pl
from jax.experimental.pallas import tpu as pltpu

F32 = jnp.float32
BF16 = jnp.bfloat16

N_HEADS = 8
N_KV_HEADS = 2
HEAD_DIM = 64
Q_PER_KV = N_HEADS // N_KV_HEADS
W_ATTN = N_HEADS * HEAD_DIM
WINDOW = 128
GRID_W = 64
ROPE_BASE = 10000.0
W_S5 = 256
S5_GROUP = 16
S5_GROUPS = W_S5 // S5_GROUP
S5_STATE = 64
S5_MODES = S5_GROUPS * S5_STATE
ML_HEADS = 4
ML_HEAD_DIM = 64
W_ML = ML_HEADS * ML_HEAD_DIM
N_GATES = 4 * ML_HEADS
N_BRANCH = 3
NORM_EPS = 1e-6

LANES = 128
BF16_SUBLANES = 16
V7X_VMEM_LIMIT_BYTES = 56 * 1024 * 1024

ROW_TILE = 1024
FFN_TILE = 1024
ATTN_TILE = 256
ML_CHUNK = 256
ML_BATCH = 4
S5_TILE = 128
FFN_CHUNK = 256
NEG_BIG = -1e30
LOG2E = math.log2(math.e)
Q_SCALE = LOG2E * HEAD_DIM ** -0.5

Z_QA = 0
Z_KV = Z_QA + W_ATTN
Z_US = Z_KV + 4 * LANES
Z_QKVM = Z_US + W_S5
Z_OM = Z_QKVM + 3 * W_ML
Z_GB = Z_OM + W_ML


def _sigmoid(x):
    return 1.0 / (1.0 + jnp.exp(-x))


def _log_sigmoid(x):
    return jnp.minimum(x, 0.0) - jnp.log(1.0 + jnp.exp(-jnp.abs(x)))


def _gelu_tanh(x):
    return 0.5 * x * (1.0 + jnp.tanh(math.sqrt(2.0 / math.pi) * (x + 0.044715 * (x * x * x))))


def _rms(x):
    return x * lax.rsqrt(jnp.mean(x * x, axis=-1, keepdims=True) + NORM_EPS)


def _dot(a, b):
    return jnp.dot(a, b, preferred_element_type=F32)


def _dot_nt(a, b):
    return lax.dot_general(a, b, (((1,), (1,)), ((), ())), preferred_element_type=F32)


def _params(sem, vmem=None):
    return pltpu.CompilerParams(dimension_semantics=sem, vmem_limit_bytes=vmem)


def _resident(shape):
    return pl.BlockSpec(shape, lambda *_: (0,) * len(shape), pipeline_mode=pl.Buffered(1))


def _mod_kernel(c_ref, w_ref, b_ref, o_ref):
    c = c_ref[...]
    s = (c * _sigmoid(c)).astype(BF16)
    o_ref[0] = _dot(s, w_ref[0]) + b_ref[0]


def _mod_call(c_all, mod_w, mod_b):
    n_layer, d, d6 = mod_w.shape
    tn = d6 // 4
    rows = c_all.shape[0]
    return pl.pallas_call(
        _mod_kernel,
        grid=(n_layer, d6 // tn),
        in_specs=[
            pl.BlockSpec((rows, d), lambda l, j: (0, 0)),
            pl.BlockSpec((1, d, tn), lambda l, j: (l, 0, j)),
            pl.BlockSpec((1, 1, tn), lambda l, j: (l, 0, j)),
        ],
        out_specs=pl.BlockSpec((1, rows, tn), lambda l, j: (l, 0, j)),
        out_shape=jax.ShapeDtypeStruct((n_layer, rows, d6), F32),
        compiler_params=_params(("parallel", "parallel")),
        name="modulation",
    )(c_all, mod_w, mod_b.reshape(n_layer, 1, d6))


def _rope(xb, cos, s1, s2):
    return xb * cos + pltpu.roll(xb, LANES - 16, 1) * s1 + pltpu.roll(xb, 16, 1) * s2


N_IN_OUT = 9


def _in_kernel(x_ref, mod_ref, g_ref, w_ref, wt_ref, *rest, d_model, rotary):
    d = d_model
    if rotary:
        cos, s1, s2 = (r[...] for r in rest[:3])
        rest = rest[3:]
    qa_ref, kv_ref, us_ref, qkvm_ref, om_ref, gb_ref, gm_ref, gmt_ref, kmt_ref = rest[-N_IN_OUT:]
    k_scale = ML_HEAD_DIM ** -0.5
    mod = mod_ref[0]
    h = (_rms(x_ref[0]) * g_ref[...] * (1.0 + mod[:, d:2 * d]) + mod[:, 0:d]).astype(BF16)

    def proj(lo, width):
        return _dot(h, w_ref[:, lo:lo + width])

    def rot(blk):
        return _rope(blk, cos, s1, s2) if rotary else blk

    zq = proj(Z_QA, W_ATTN)
    for p in range(W_ATTN // LANES):
        sl = slice(p * LANES, (p + 1) * LANES)
        qa_ref[0, :, sl] = (rot(zq[:, sl]) * Q_SCALE).astype(BF16)
    zkv = proj(Z_KV, 4 * LANES)
    for p in range(4):
        sl = slice(p * LANES, (p + 1) * LANES)
        kv_ref[0, :, sl] = (rot(zkv[:, sl]) if p < 2 else zkv[:, sl]).astype(BF16)
    us_ref[0] = proj(Z_US, W_S5)
    zm = proj(Z_QKVM, 3 * W_ML)
    qkvm_ref[0, :, 0:W_ML] = zm[:, 0:W_ML].astype(BF16)
    qkvm_ref[0, :, W_ML:2 * W_ML] = (zm[:, W_ML:2 * W_ML] * k_scale).astype(BF16)
    qkvm_ref[0, :, 2 * W_ML:3 * W_ML] = zm[:, 2 * W_ML:3 * W_ML].astype(BF16)
    om_ref[0] = proj(Z_OM, W_ML)
    for p in range(N_BRANCH):
        gb_ref[0, :, p * d:(p + 1) * d] = proj(Z_GB + p * d, d).astype(BF16)
    gm_ref[0] = proj(Z_GB + N_BRANCH * d, LANES)
    zt = _dot_nt(wt_ref[...], h)
    gmt_ref[0] = zt[0:N_GATES]
    kmt_ref[0] = (zt[N_GATES:N_GATES + W_ML] * k_scale).astype(BF16)


def _in_call(x, modsel, mod_off, norm_g, w_in, w_t, tc, tile, blk_off, rope=None, prev=None):
    b, n, d = x.shape
    nz = w_in.shape[1]
    widths = (W_ATTN, 4 * LANES, W_S5, 3 * W_ML, W_ML, N_BRANCH * d, LANES)
    dtypes = (BF16, BF16, F32, BF16, F32, BF16, F32)
    out_shapes = [jax.ShapeDtypeStruct((b, tc, w), t) for w, t in zip(widths, dtypes)]
    out_shapes.append(jax.ShapeDtypeStruct((b, N_GATES, tc), F32))
    out_shapes.append(jax.ShapeDtypeStruct((b, W_ML, tc), BF16))
    out_specs = [pl.BlockSpec((1, tile, w), lambda bi, i: (bi, i + blk_off, 0)) for w in widths]
    out_specs.append(pl.BlockSpec((1, N_GATES, tile), lambda bi, i: (bi, 0, i + blk_off)))
    out_specs.append(pl.BlockSpec((1, W_ML, tile), lambda bi, i: (bi, 0, i + blk_off)))
    in_specs = [
        pl.BlockSpec((1, tile, d), lambda bi, i: (bi, i, 0)),
        pl.BlockSpec((1, 1, 6 * d), lambda bi, i: (mod_off + bi, 0, 0)),
        _resident((1, d)),
        _resident((d, nz)),
        _resident((N_GATES + W_ML, d)),
    ]
    args = [x, modsel, norm_g.reshape(1, d), w_in, w_t]
    if rope is not None:
        in_specs += [pl.BlockSpec((tile, LANES), lambda bi, i: (i, 0))] * 3
        args += list(rope)
    aliases = {}
    if prev is not None:
        aliases = {len(args) + k: k for k in range(N_IN_OUT)}
        in_specs += [pl.BlockSpec(memory_space=pl.ANY)] * N_IN_OUT
        args += list(prev)
    return pl.pallas_call(
        functools.partial(_in_kernel, d_model=d, rotary=rope is not None),
        grid=(b, n // tile),
        in_specs=in_specs,
        out_specs=out_specs,
        out_shape=out_shapes,
        input_output_aliases=aliases,
        compiler_params=_params(("parallel", "parallel"), V7X_VMEM_LIMIT_BYTES),
        name="in_proj",
    )(*args)


def _attn_heads(sink_ref, q_ref, kv_scr, o_ref, bias, n_key):
    lane = lax.broadcasted_iota(jnp.int32, (n_key, LANES), 1)
    n_q = q_ref.shape[1]
    own = lax.broadcasted_iota(jnp.int32, (n_q, LANES), 1) < HEAD_DIM
    first = lax.broadcasted_iota(jnp.int32, (2 * n_q, 1), 0) < n_q
    n_bias = 0 if bias is None else bias.shape[1]
    if n_bias:
        bias = jnp.concatenate([bias, bias], axis=0)
    for kvh in range(N_KV_HEADS):
        kd = kv_scr[0:n_key, kvh * LANES:(kvh + 1) * LANES]
        vd = kv_scr[0:n_key, (N_KV_HEADS + kvh) * LANES:(N_KV_HEADS + kvh + 1) * LANES]
        blocks = (2 * kvh, 2 * kvh + 1)
        qb = jnp.concatenate([q_ref[0, :, p * LANES:(p + 1) * LANES] for p in blocks], axis=0)
        outs = []
        for par in range(2):
            half = (lane >= HEAD_DIM) if par else (lane < HEAD_DIM)
            km = jnp.where(half, kd, jnp.zeros_like(kd))
            vm = jnp.where(half, vd, jnp.ones_like(vd))
            s = _dot_nt(qb, km)
            parts = [s[:, 0:n_bias] + bias, s[:, n_bias:n_key]] if n_bias else [s]
            sk = jnp.where(first, sink_ref[2 * blocks[0] + par], sink_ref[2 * blocks[1] + par]) * LOG2E
            m = sk
            for part in parts:
                m = jnp.maximum(m, jnp.max(part, axis=1, keepdims=True))
            e = jnp.concatenate([jnp.exp2(part - m).astype(BF16) for part in parts], axis=1)
            o = _dot(e, vm)
            den = pltpu.roll(o, HEAD_DIM, 1) + jnp.exp2(sk - m)
            outs.append(o / den)
        for j, p in enumerate(blocks):
            rows = slice(j * n_q, (j + 1) * n_q)
            o_ref[0, :, p * LANES:(p + 1) * LANES] = jnp.where(own, outs[0][rows], outs[1][rows]).astype(BF16)


def _attn_latent_kernel(sink_ref, q_ref, kvp_ref, kvm_ref, kvn_ref, kvc_ref, o_ref, kv_scr, *, n_tok, n_ctx):
    i = pl.program_id(1)
    n_loc = ATTN_TILE + 2 * WINDOW
    n_key = n_loc + n_ctx
    kv_scr[0:WINDOW] = kvp_ref[0]
    kv_scr[WINDOW:WINDOW + ATTN_TILE] = kvm_ref[0]
    kv_scr[WINDOW + ATTN_TILE:n_loc] = kvn_ref[0]
    kv_scr[n_loc:n_key] = kvc_ref[0]
    t0 = i * ATTN_TILE
    row = lax.broadcasted_iota(jnp.int32, (ATTN_TILE, n_loc), 0)
    col = lax.broadcasted_iota(jnp.int32, (ATTN_TILE, n_loc), 1)
    lo = jnp.maximum(row, WINDOW - t0)
    hi = jnp.minimum(row + 2 * WINDOW, n_tok - t0 + WINDOW - 1)
    valid = jnp.logical_and(col >= lo, col <= hi)
    _attn_heads(sink_ref, q_ref, kv_scr, o_ref, jnp.where(valid, 0.0, NEG_BIG), n_key)


def _attn_context_kernel(sink_ref, q_ref, kvc_ref, o_ref, kv_scr, *, n_ctx):
    kv_scr[...] = kvc_ref[0]
    _attn_heads(sink_ref, q_ref, kv_scr, o_ref, None, n_ctx)


def _attn_calls(sink, qa, kv, n_tok, n_ctx, with_ctx):
    b = qa.shape[0]
    kvw = kv.shape[2]
    per = ATTN_TILE // WINDOW
    ctx_blk = n_tok // n_ctx
    last_half = (n_tok + n_ctx) // WINDOW - 1
    smem = pl.BlockSpec(memory_space=pltpu.SMEM)
    ya = pl.pallas_call(
        functools.partial(_attn_latent_kernel, n_tok=n_tok, n_ctx=n_ctx),
        grid=(b, n_tok // ATTN_TILE),
        in_specs=[
            smem,
            pl.BlockSpec((1, ATTN_TILE, W_ATTN), lambda bi, i: (bi, i, 0)),
            pl.BlockSpec((1, WINDOW, kvw), lambda bi, i: (bi, jnp.maximum(per * i - 1, 0), 0)),
            pl.BlockSpec((1, ATTN_TILE, kvw), lambda bi, i: (bi, i, 0)),
            pl.BlockSpec((1, WINDOW, kvw), lambda bi, i: (bi, jnp.minimum(per * i + per, last_half), 0)),
            pl.BlockSpec((1, n_ctx, kvw), lambda bi, i: (bi, ctx_blk, 0)),
        ],
        out_specs=pl.BlockSpec((1, ATTN_TILE, W_ATTN), lambda bi, i: (bi, i, 0)),
        out_shape=jax.ShapeDtypeStruct((b, n_tok, W_ATTN), BF16),
        scratch_shapes=[pltpu.VMEM((ATTN_TILE + 2 * WINDOW + n_ctx, kvw), BF16)],
        compiler_params=_params(("parallel", "parallel"), V7X_VMEM_LIMIT_BYTES),
        name="attention",
    )(sink, qa, kv, kv, kv, kv)
    if not with_ctx:
        return ya, None
    ya_ctx = pl.pallas_call(
        functools.partial(_attn_context_kernel, n_ctx=n_ctx),
        grid=(b,),
        in_specs=[
            smem,
            pl.BlockSpec((1, n_ctx, W_ATTN), lambda bi: (bi, ctx_blk, 0)),
            pl.BlockSpec((1, n_ctx, kvw), lambda bi: (bi, ctx_blk, 0)),
        ],
        out_specs=pl.BlockSpec((1, n_ctx, W_ATTN), lambda bi: (bi, 0, 0)),
        out_shape=jax.ShapeDtypeStruct((b, n_ctx, W_ATTN), BF16),
        scratch_shapes=[pltpu.VMEM((n_ctx, kvw), BF16)],
        compiler_params=_params(("parallel",)),
        name="attention_ctx",
    )(sink, qa, kv)
    return ya, ya_ctx


def _scan_tile(step, n_lat, n_ctx_tiles, reverse):
    if reverse:
        return jnp.where(step < n_ctx_tiles, n_lat + n_ctx_tiles - 1 - step, n_lat - 1 - (step - n_ctx_tiles))
    return jnp.where(step < n_ctx_tiles, n_lat + step, step - n_ctx_tiles)


def _flip_tokens(a, n_batch):
    n = a.shape[0] // n_batch
    return jnp.concatenate([a[(n - 1 - j) * n_batch:(n - j) * n_batch] for j in range(n)], axis=0)


def _s5_scan_tile(u_ref, bblk_ref, cblk_ref, lam_ref, x_scr, st_scr, *, reverse, n_batch):
    i = pl.program_id(0)

    @pl.when(i == 0)
    def _():
        st_scr[...] = jnp.zeros_like(st_scr)

    u = u_ref[...]
    x_scr[...] = _dot((_flip_tokens(u, n_batch) if reverse else u).astype(BF16), bblk_ref[...])
    lre = jnp.broadcast_to(lam_ref[0:1, :], (n_batch, S5_MODES))
    lim = jnp.broadcast_to(lam_ref[1:2, :], (n_batch, S5_MODES))

    re, im = st_scr[0], st_scr[1]
    for j in range(S5_TILE):
        r0 = j * n_batch
        bre = x_scr[r0:r0 + n_batch, 0:S5_MODES]
        bim = x_scr[r0:r0 + n_batch, S5_MODES:2 * S5_MODES]
        re, im = lre * re - lim * im + bre, lre * im + lim * re + bim
        x_scr[r0:r0 + n_batch, 0:S5_MODES] = re
        x_scr[r0:r0 + n_batch, S5_MODES:2 * S5_MODES] = im
    st_scr[0] = re
    st_scr[1] = im
    y = _dot(x_scr[...].astype(BF16), cblk_ref[...])
    return u, (_flip_tokens(y, n_batch) if reverse else y)


def _s5_fwd_kernel(u_ref, bblk_ref, cblk_ref, lam_ref, o_ref, x_scr, st_scr, *, n_batch):
    _, y = _s5_scan_tile(u_ref, bblk_ref, cblk_ref, lam_ref, x_scr, st_scr, reverse=False, n_batch=n_batch)
    o_ref[...] = y


def _s5_bwd_kernel(u_ref, bblk_ref, cblk_ref, lam_ref, yf_ref, dsk_ref, wg_ref, bg_ref, o_ref, x_scr, st_scr,
                   *, n_batch):
    u, y = _s5_scan_tile(u_ref, bblk_ref, cblk_ref, lam_ref, x_scr, st_scr, reverse=True, n_batch=n_batch)
    y = _gelu_tanh(y + yf_ref[...] + dsk_ref[...] * u)
    o_ref[...] = (y * _sigmoid(_dot(y.astype(BF16), wg_ref[...]) + bg_ref[...])).astype(BF16)


def _s5_calls(u_rows, prm, n_batch, n_tok, n_ctx):
    rows_total = u_rows.shape[0]
    tile_rows = S5_TILE * n_batch
    nt = rows_total // tile_rows
    n_lat, nct = n_tok // S5_TILE, n_ctx // S5_TILE

    def fwd_tile(i):
        return (_scan_tile(i, n_lat, nct, False), 0)

    def bwd_tile(i):
        return (_scan_tile(i, n_lat, nct, True), 0)

    scratch = [pltpu.VMEM((tile_rows, 2 * S5_MODES), F32), pltpu.VMEM((2, n_batch, S5_MODES), F32)]
    wspecs = [_resident((W_S5, 2 * S5_MODES)), _resident((2 * S5_MODES, W_S5)), _resident((2, S5_MODES))]
    yf = pl.pallas_call(
        functools.partial(_s5_fwd_kernel, n_batch=n_batch),
        grid=(nt,),
        in_specs=[pl.BlockSpec((tile_rows, W_S5), fwd_tile)] + wspecs,
        out_specs=pl.BlockSpec((tile_rows, W_S5), fwd_tile),
        out_shape=jax.ShapeDtypeStruct((rows_total, W_S5), F32),
        scratch_shapes=scratch,
        compiler_params=_params(("arbitrary",), V7X_VMEM_LIMIT_BYTES),
        name="s5_forward",
    )(u_rows, prm["bblk"][0], prm["cblk"], prm["lam"][0])
    return pl.pallas_call(
        functools.partial(_s5_bwd_kernel, n_batch=n_batch),
        grid=(nt,),
        in_specs=[pl.BlockSpec((tile_rows, W_S5), bwd_tile)] + wspecs + [
            pl.BlockSpec((tile_rows, W_S5), bwd_tile),
            _resident((1, W_S5)),
            _resident((W_S5, W_S5)),
            _resident((1, W_S5)),
        ],
        out_specs=pl.BlockSpec((tile_rows, W_S5), bwd_tile),
        out_shape=jax.ShapeDtypeStruct((rows_total, W_S5), BF16),
        scratch_shapes=scratch,
        compiler_params=_params(("arbitrary",), V7X_VMEM_LIMIT_BYTES),
        name="s5_backward",
    )(u_rows, prm["bblk"][1], prm["cblk"], prm["lam"][1], yf, prm["dsk"], prm["wglu"], prm["bglu"])


def _s5_prep(a_re, a_im, log_dt, b_re, b_im, c_re, c_im, d_skip, glu_w, glu_b):
    eye = jnp.eye(S5_GROUPS, dtype=F32)
    bblk, lam = [], []
    for d in range(2):
        dt = jnp.exp(log_dt[d])[:, None]
        dre, dim = dt * a_re[d], dt * a_im[d]
        mag = jnp.exp(dre)
        lam_re, lam_im = mag * jnp.cos(dim), mag * jnp.sin(dim)
        den = a_re[d] * a_re[d] + a_im[d] * a_im[d]
        coef_re = ((lam_re - 1.0) * a_re[d] + lam_im * a_im[d]) / den
        coef_im = (lam_im * a_re[d] - (lam_re - 1.0) * a_im[d]) / den
        cr, ci = coef_re[..., None], coef_im[..., None]
        bb_re = cr * b_re - ci * b_im
        bb_im = cr * b_im + ci * b_re
        blk = [jnp.einsum("gpc,gh->gchp", bb, eye).reshape(W_S5, S5_MODES) for bb in (bb_re, bb_im)]
        bblk.append(jnp.concatenate(blk, axis=1).astype(BF16))
        lam.append(jnp.stack([lam_re.reshape(-1), lam_im.reshape(-1)]))
    cblk = [jnp.einsum("gcp,gh->gphc", cc, eye).reshape(S5_MODES, W_S5) for cc in (c_re, -c_im)]
    return dict(bblk=bblk, lam=lam, cblk=jnp.concatenate(cblk, axis=0).astype(BF16),
                dsk=d_skip.reshape(1, W_S5), wglu=glu_w.astype(BF16), bglu=glu_b.reshape(1, W_S5))


def _head_blocks(cols):
    shape = (cols[0].shape[0], LANES)
    first = lax.broadcasted_iota(jnp.int32, shape, 1) < ML_HEAD_DIM
    return jnp.concatenate([jnp.where(first, cols[0], cols[1]), jnp.where(first, cols[2], cols[3])], axis=1)


def _split3(x):
    hi = x.astype(BF16)
    rem = x - hi.astype(F32)
    mid = rem.astype(BF16)
    return hi, mid, (rem - mid.astype(F32)).astype(BF16)


def _select_cols(x, sel, terms=3):
    return sum(_dot(part, sel) for part in _split3(x)[:terms])


def _mlstm_chunk(bb, qkv_ref, kmt_ref, gm_ref, gmt_ref, brow_ref, bcol_ref, ct_scr, n_scr, mc_scr, mr_scr,
                 *, reverse):
    n = ML_CHUNK
    g0 = 2 * ML_HEADS * (1 if reverse else 0)
    last = 0 if reverse else n - 1
    q = qkv_ref[bb, :, 0:W_ML]
    k = qkv_ref[bb, :, W_ML:2 * W_ML]
    v = qkv_ref[bb, :, 2 * W_ML:3 * W_ML]
    ti = lax.broadcasted_iota(jnp.int32, (n, n), 0)
    si = lax.broadcasted_iota(jnp.int32, (n, n), 1)
    causal = (si >= ti) if reverse else (si <= ti)
    tri = jnp.where(causal, 1.0, 0.0).astype(BF16)
    tri_t = jnp.where((ti >= si) if reverse else (ti <= si), 1.0, 0.0).astype(BF16)

    g_col = gm_ref[bb] + brow_ref[...]
    lf_col = _log_sigmoid(pltpu.roll(g_col, LANES - ML_HEADS, 1))
    bcum = sum(_dot(tri, part) for part in _split3(lf_col))
    r_col = g_col - bcum
    rowi = lax.broadcasted_iota(jnp.int32, (n, LANES), 0)
    cmax = r_col
    sh = 1
    while sh < n:
        if reverse:
            cmax = jnp.where(rowi < n - sh, jnp.maximum(cmax, pltpu.roll(cmax, n - sh, 0)), cmax)
        else:
            cmax = jnp.where(rowi >= sh, jnp.maximum(cmax, pltpu.roll(cmax, sh, 0)), cmax)
        sh *= 2
    m_col = mc_scr[bb, 0:1, :]
    mm = jnp.maximum(m_col, cmax)
    mm_last = mm[last:last + 1, :]
    mc_scr[bb, 0:1, :] = bcum[last:last + 1, :] + mm_last
    used = jnp.logical_and(lax.broadcasted_iota(jnp.int32, (n, LANES), 1) >= g0,
                           lax.broadcasted_iota(jnp.int32, (n, LANES), 1) < g0 + ML_HEADS)

    src = lax.broadcasted_iota(jnp.int32, (LANES, W_ML), 0)
    dst = lax.broadcasted_iota(jnp.int32, (LANES, W_ML), 1) // ML_HEAD_DIM
    sel_head = jnp.where(src == g0 + dst, 1.0, 0.0).astype(BF16)
    wi_b = _select_cols(jnp.where(used, jnp.exp(m_col - mm), 0.0), sel_head, terms=2)
    floor_b = jnp.exp(_select_cols(jnp.where(used, -(bcum + mm), 0.0), sel_head))

    g_row = gmt_ref[bb, g0:g0 + 2 * ML_HEADS, :] + bcol_ref[g0:g0 + 2 * ML_HEADS, 0:1]
    lf_row = _log_sigmoid(g_row)
    bcum_row = sum(_dot(part, tri_t) for part in _split3(lf_row))
    r_row = g_row[0:ML_HEADS] - bcum_row[ML_HEADS:2 * ML_HEADS]
    m_row = mr_scr[bb, 0:ML_HEADS, :]
    mm_row = jnp.maximum(m_row, jnp.max(r_row, axis=1, keepdims=True))
    decay_row = jnp.exp(m_row - mm_row)
    mr_scr[bb, 0:ML_HEADS, :] = jnp.sum(lf_row[ML_HEADS:2 * ML_HEADS], axis=1, keepdims=True) + mm_row
    w_state = jnp.exp(r_row - mm_row[:, 0:1])
    decay_col = jnp.concatenate(
        [jnp.broadcast_to(decay_row[h:h + 1, :], (ML_HEAD_DIM, LANES)) for h in range(ML_HEADS)], axis=0)
    w_state_rows = jnp.concatenate(
        [jnp.broadcast_to(w_state[h:h + 1, :], (ML_HEAD_DIM, n)) for h in range(ML_HEADS)], axis=0)

    lane_head = lax.broadcasted_iota(jnp.int32, (n, W_ML), 1) // ML_HEAD_DIM
    head_mask = [jnp.where(lane_head == h, 1.0, 0.0).astype(BF16) for h in range(ML_HEADS)]
    kexp = jnp.concatenate([k * mk for mk in head_mask], axis=0)
    s_all = _dot_nt(q, kexp)
    own = lax.broadcasted_iota(jnp.int32, (n, LANES), 1) < ML_HEAD_DIM
    own_b = jnp.where(own, 1.0, 0.0).astype(BF16)
    other_b = jnp.where(own, 0.0, 1.0).astype(BF16)
    per_head = []
    for h in range(ML_HEADS):
        mm_b = jnp.broadcast_to(mm[:, g0 + h:g0 + h + 1], (n, n))
        arg = jnp.where(causal, r_row[h:h + 1, :] - mm_b, NEG_BIG)
        w = (jnp.exp(arg) * s_all[:, h * n:(h + 1) * n]).astype(BF16)
        v_pair = v[:, (h // 2) * LANES:(h // 2 + 1) * LANES]
        keep, fill = (own_b, other_b) if h % 2 == 0 else (other_b, own_b)
        per_head.append(_dot(w, v_pair * keep + fill))
    pairs = [(per_head[2 * p], per_head[2 * p + 1]) for p in range(ML_HEADS // 2)]
    num_intra = jnp.concatenate([jnp.where(own, even, odd) for even, odd in pairs], axis=1)
    sum_intra = jnp.concatenate(
        [pltpu.roll(jnp.where(own, odd, even), ML_HEAD_DIM, 1) for even, odd in pairs], axis=1)

    ct = ct_scr[bb]
    nst = n_scr[bb]
    bi = lax.broadcasted_iota(jnp.int32, (W_ML, W_ML), 0) // ML_HEAD_DIM
    bj = lax.broadcasted_iota(jnp.int32, (W_ML, W_ML), 1) // ML_HEAD_DIM
    same_head = bi == bj
    nblk = jnp.where(same_head, jnp.concatenate([nst, nst], axis=1), 0.0)
    num = wi_b * _dot(q, ct.astype(BF16)) + num_intra
    den = wi_b * _dot(q, nblk.astype(BF16)) + sum_intra
    h_out = num / jnp.maximum(jnp.abs(den), floor_b)

    ktw = (kmt_ref[bb].astype(F32) * w_state_rows).astype(BF16)
    vext = jnp.concatenate([v, jnp.ones((n, LANES), BF16)], axis=1)
    upd = _dot(ktw, vext)
    ct_scr[bb] = jnp.concatenate([decay_col, decay_col], axis=1) * ct + jnp.where(same_head, upd[:, 0:W_ML], 0.0)
    n_scr[bb] = decay_col * nst + upd[:, W_ML:W_ML + LANES]
    return h_out


def _mlstm_reset(ct_scr, n_scr, mc_scr, mr_scr):
    @pl.when(pl.program_id(1) == 0)
    def _():
        ct_scr[...] = jnp.zeros_like(ct_scr)
        n_scr[...] = jnp.zeros_like(n_scr)
        mc_scr[...] = jnp.zeros_like(mc_scr)
        mr_scr[...] = jnp.zeros_like(mr_scr)


def _mlstm_fwd_kernel(qkv_ref, kmt_ref, gm_ref, gmt_ref, brow_ref, bcol_ref, hf_ref, *state):
    _mlstm_reset(*state)
    for bb in range(ML_BATCH):
        hf_ref[bb] = _mlstm_chunk(bb, qkv_ref, kmt_ref, gm_ref, gmt_ref, brow_ref, bcol_ref, *state, reverse=False)


def _mlstm_bwd_kernel(qkv_ref, kmt_ref, gm_ref, gmt_ref, brow_ref, bcol_ref, hf_ref, om_ref, ng_ref, o_ref, *state):
    _mlstm_reset(*state)
    lane_head = lax.broadcasted_iota(jnp.int32, (ML_CHUNK, W_ML), 1) // ML_HEAD_DIM
    for bb in range(ML_BATCH):
        ht = hf_ref[bb] + _mlstm_chunk(bb, qkv_ref, kmt_ref, gm_ref, gmt_ref, brow_ref, bcol_ref, *state,
                                       reverse=True)
        sq = ht * ht
        ms = [jnp.sum(jnp.where(lane_head == h, sq, 0.0), axis=1, keepdims=True) * (1.0 / ML_HEAD_DIM)
              for h in range(ML_HEADS)]
        hn = ht * lax.rsqrt(_head_blocks(ms) + NORM_EPS) * ng_ref[...]
        o_ref[bb] = (hn * _sigmoid(om_ref[bb])).astype(BF16)


def _mlstm_calls(qkvm, kmt, gm, gmt, om, bias_row, bias_col, norm_g, n_tok, n_ctx):
    b, tc, _ = qkvm.shape
    nt = tc // ML_CHUNK
    n_lat, nct = n_tok // ML_CHUNK, n_ctx // ML_CHUNK
    nb = ML_BATCH
    state = [
        pltpu.VMEM((nb, W_ML, W_ML), F32),
        pltpu.VMEM((nb, W_ML, LANES), F32),
        pltpu.VMEM((nb, 8, LANES), F32),
        pltpu.VMEM((nb, 8, LANES), F32),
    ]

    def specs(reverse):
        def tile(g, s):
            return (g, _scan_tile(s, n_lat, nct, reverse), 0)

        def tile_t(g, s):
            return (g, 0, _scan_tile(s, n_lat, nct, reverse))

        row = lambda w: pl.BlockSpec((nb, ML_CHUNK, w), tile)
        common = [row(3 * W_ML), pl.BlockSpec((nb, W_ML, ML_CHUNK), tile_t), row(LANES),
                  pl.BlockSpec((nb, N_GATES, ML_CHUNK), tile_t), _resident((1, LANES)), _resident((N_GATES, LANES))]
        return common, row

    common, row = specs(False)
    hf = pl.pallas_call(
        _mlstm_fwd_kernel,
        grid=(b // nb, nt),
        in_specs=common,
        out_specs=row(W_ML),
        out_shape=jax.ShapeDtypeStruct((b, tc, W_ML), F32),
        scratch_shapes=state,
        compiler_params=_params(("parallel", "arbitrary"), V7X_VMEM_LIMIT_BYTES),
        name="mlstm_forward",
    )(qkvm, kmt, gm, gmt, bias_row, bias_col)
    common, row = specs(True)
    return pl.pallas_call(
        _mlstm_bwd_kernel,
        grid=(b // nb, nt),
        in_specs=common + [row(W_ML), row(W_ML), _resident((1, W_ML))],
        out_specs=row(W_ML),
        out_shape=jax.ShapeDtypeStruct((b, tc, W_ML), BF16),
        scratch_shapes=state,
        compiler_params=_params(("parallel", "arbitrary"), V7X_VMEM_LIMIT_BYTES),
        name="mlstm_backward",
    )(qkvm, kmt, gm, gmt, bias_row, bias_col, hf, om, norm_g.reshape(1, W_ML))


def _merge_kernel(ya_ref, ys_ref, ym_ref, gb_ref, x_ref, mod_ref, g_ref, wa_ref, ws_ref, wm_ref, wo_ref,
                  x1_ref, h2_ref, *, d_model):
    d = d_model
    mod = mod_ref[0]
    y = (_sigmoid(gb_ref[0, :, 0:d].astype(F32)) * _dot(ya_ref[0], wa_ref[...])
         + _sigmoid(gb_ref[0, :, d:2 * d].astype(F32)) * _dot(ys_ref[0], ws_ref[...])
         + _sigmoid(gb_ref[0, :, 2 * d:3 * d].astype(F32)) * _dot(ym_ref[0], wm_ref[...]))
    x1 = x_ref[0] + mod[:, 2 * d:3 * d] * _dot(y.astype(BF16), wo_ref[...])
    x1_ref[0] = x1
    h2_ref[0] = (_rms(x1) * g_ref[...] * (1.0 + mod[:, 4 * d:5 * d]) + mod[:, 3 * d:4 * d]).astype(BF16)


def _merge_call(ya, ys, ym, gb, x, modsel, mod_off, norm_g, wa, ws, wm, wo, tile, blk_off):
    b, n, d = x.shape

    def own(bi, i):
        return (bi, i, 0)

    def shared(bi, i):
        return (bi, i + blk_off, 0)

    return pl.pallas_call(
        functools.partial(_merge_kernel, d_model=d),
        grid=(b, n // tile),
        in_specs=[
            pl.BlockSpec((1, tile, W_ATTN), own),
            pl.BlockSpec((1, tile, W_S5), shared),
            pl.BlockSpec((1, tile, W_ML), shared),
            pl.BlockSpec((1, tile, N_BRANCH * d), shared),
            pl.BlockSpec((1, tile, d), own),
            pl.BlockSpec((1, 1, 6 * d), lambda bi, i: (mod_off + bi, 0, 0)),
            _resident((1, d)),
            _resident((W_ATTN, d)),
            _resident((W_S5, d)),
            _resident((W_ML, d)),
            _resident((d, d)),
        ],
        out_specs=[pl.BlockSpec((1, tile, d), own), pl.BlockSpec((1, tile, d), own)],
        out_shape=(jax.ShapeDtypeStruct((b, n, d), F32), jax.ShapeDtypeStruct((b, n, d), BF16)),
        compiler_params=_params(("parallel", "parallel"), V7X_VMEM_LIMIT_BYTES),
        name="merge",
    )(ya, ys, ym, gb, x, modsel, norm_g.reshape(1, d), wa, ws, wm, wo)


def _ffn_kernel(hp_ref, hm_ref, hn_ref, x1_ref, mod_ref, wup_ref, cw_ref, cb_ref, wdn_ref, fg_ref, o_ref,
                lhs_scr, act_scr, *, d_model, d_ff, tile, final):
    d = d_model
    halo = BF16_SUBLANES
    rows = tile + 2 * halo
    i = pl.program_id(1)
    prev_ok = i >= 1
    next_ok = i < pl.num_programs(1) - 1
    lhs_scr[0:halo] = jnp.where(prev_ok, hp_ref[0], jnp.zeros_like(hp_ref[0]))
    lhs_scr[halo:halo + tile] = hm_ref[0]
    lhs_scr[halo + tile:rows] = jnp.where(next_ok, hn_ref[0], jnp.zeros_like(hn_ref[0]))
    lhs = lhs_scr[...]
    for j in range(d_ff // FFN_CHUNK):
        parts = []
        for lo in (j * FFN_CHUNK, d_ff + j * FFN_CHUNK):
            u = _dot(lhs, wup_ref[:, lo:lo + FFN_CHUNK])
            cw = cw_ref[:, lo:lo + FFN_CHUNK]
            parts.append(pltpu.roll(u, 1, 0)[halo:halo + tile] * cw[0:1]
                         + u[halo:halo + tile] * cw[1:2]
                         + pltpu.roll(u, rows - 1, 0)[halo:halo + tile] * cw[2:3]
                         + cb_ref[:, lo:lo + FFN_CHUNK])
        a, v = parts
        act_scr[:, j * FFN_CHUNK:(j + 1) * FFN_CHUNK] = (a * _sigmoid(a) * v).astype(BF16)
    out = x1_ref[0] + mod_ref[0][:, 5 * d:6 * d] * _dot(act_scr[...], wdn_ref[...])
    if final:
        out = _rms(out) * fg_ref[...]
    o_ref[0] = out


def _ffn_call(h2, x1, modsel, mod_off, w_up, conv_w, conv_b, w_down, final_g, tile, *, final):
    b, n, d = x1.shape
    d_ff = w_down.shape[0]
    halo = BF16_SUBLANES
    per = tile // halo
    n_halo = n // halo
    return pl.pallas_call(
        functools.partial(_ffn_kernel, d_model=d, d_ff=d_ff, tile=tile, final=final),
        grid=(b, n // tile),
        in_specs=[
            pl.BlockSpec((1, halo, d), lambda bi, i: (bi, jnp.maximum(i * per - 1, 0), 0)),
            pl.BlockSpec((1, tile, d), lambda bi, i: (bi, i, 0)),
            pl.BlockSpec((1, halo, d), lambda bi, i: (bi, jnp.minimum((i + 1) * per, n_halo - 1), 0)),
            pl.BlockSpec((1, tile, d), lambda bi, i: (bi, i, 0)),
            pl.BlockSpec((1, 1, 6 * d), lambda bi, i: (mod_off + bi, 0, 0)),
            _resident((d, 2 * d_ff)),
            _resident((3, 2 * d_ff)),
            _resident((1, 2 * d_ff)),
            _resident((d_ff, d)),
            _resident((1, d)),
        ],
        out_specs=pl.BlockSpec((1, tile, d), lambda bi, i: (bi, i, 0)),
        out_shape=jax.ShapeDtypeStruct((b, n, d), F32),
        scratch_shapes=[pltpu.VMEM((tile + 2 * halo, d), BF16), pltpu.VMEM((tile, d_ff), BF16)],
        compiler_params=_params(("parallel", "parallel"), V7X_VMEM_LIMIT_BYTES),
        name="conv_ffn",
    )(h2, h2, h2, x1, modsel, w_up, conv_w, conv_b.reshape(1, 2 * d_ff), w_down, final_g.reshape(1, d))


def _prep_w_in(w):
    d = w.shape[0]
    k0, k1 = w[:, W_ATTN:W_ATTN + HEAD_DIM], w[:, W_ATTN + HEAD_DIM:W_ATTN + 2 * HEAD_DIM]
    o = W_ATTN + 2 * HEAD_DIM
    v0, v1 = w[:, o:o + HEAD_DIM], w[:, o + HEAD_DIM:o + 2 * HEAD_DIM]
    o += 2 * HEAD_DIM
    rest = w[:, o:o + W_S5 + 4 * W_ML]
    o += W_S5 + 4 * W_ML
    gm = w[:, o:o + N_GATES]
    gb = w[:, o + N_GATES:o + N_GATES + N_BRANCH * d]
    km = rest[:, W_S5 + W_ML:W_S5 + 2 * W_ML]
    cols = [w[:, 0:W_ATTN], k0, k0, k1, k1, v0, v0, v1, v1, rest, gb, jnp.tile(gm, (1, LANES // N_GATES))]
    w_t = jnp.concatenate([gm, km], axis=1).T
    return jnp.concatenate(cols, axis=1).astype(BF16), w_t.astype(BF16)


def _rope_tables(n_tok):
    pos = jnp.arange(n_tok)
    row = (pos // GRID_W).astype(F32)
    col = (pos % GRID_W).astype(F32)
    n_freq = HEAD_DIM // 4
    inv_freq = ROPE_BASE ** (-jnp.arange(n_freq, dtype=F32) / n_freq)
    lane = jnp.arange(LANES)
    dd = lane % HEAD_DIM
    by_col = (dd // (HEAD_DIM // 2)) == 1
    d32 = dd % (HEAD_DIM // 2)
    first = d32 < n_freq
    freq = inv_freq[d32 % n_freq]
    ang = jnp.where(by_col[None, :], col[:, None], row[:, None]) * freq[None, :]
    cos, sin = jnp.cos(ang), jnp.sin(ang)
    return cos, jnp.where(first[None, :], -sin, 0.0), jnp.where(first[None, :], 0.0, sin)


def kernel(x, c, ctx, c_ctx, mod_w, mod_b, norm1_g, norm2_g, w_in, attn_sink, s5_a_re, s5_a_im, s5_log_dt, s5_b_re, s5_b_im, s5_c_re, s5_c_im, s5_d, s5_glu_w, s5_glu_b, ml_igate_b, ml_fgate_b, ml_norm_g, w_branch_attn, w_branch_s5, w_branch_ml, w_out, ffn_w_up, ffn_conv_w, ffn_conv_b, ffn_w_down, final_norm_g):
    b, n_tok, d = x.shape
    n_ctx = ctx.shape[1]
    n_layer = mod_w.shape[0]
    assert n_tok % ROW_TILE == 0 and n_tok % FFN_TILE == 0 and n_tok % n_ctx == 0 and n_tok % GRID_W == 0
    assert n_ctx % ATTN_TILE == 0 and n_ctx % ML_CHUNK == 0 and n_ctx % S5_TILE == 0
    assert b % 8 == 0 and b % ML_BATCH == 0 and b + 1 <= 16 and d % LANES == 0
    tc = n_tok + n_ctx

    c_all = jnp.zeros((16, d), F32).at[:b].set(c).at[b].set(c_ctx)
    mod = _mod_call(c_all, mod_w.astype(BF16), mod_b)
    rope = _rope_tables(n_tok)
    xl, xk = x, ctx

    for l in range(n_layer):
        final = l == n_layer - 1
        modsel = jnp.concatenate([jnp.broadcast_to(mod[l, b:b + 1], (b, 6 * d)), mod[l, :b]]).reshape(2 * b, 1, 6 * d)
        w_in_l, w_t = _prep_w_in(w_in[l])
        outs = _in_call(xl, modsel, b, norm1_g[l], w_in_l, w_t, tc, ROW_TILE, 0, rope=rope)
        outs = _in_call(xk, modsel, 0, norm1_g[l], w_in_l, w_t, tc, n_ctx, n_tok // n_ctx, prev=outs)
        qa, kv, us, qkvm, om, gb, gm, gmt, kmt = outs

        ya, ya_ctx = _attn_calls(attn_sink[l], qa, kv, n_tok, n_ctx, with_ctx=not final)

        s5p = _s5_prep(s5_a_re[l], s5_a_im[l], s5_log_dt[l], s5_b_re[l], s5_b_im[l], s5_c_re[l], s5_c_im[l],
                       s5_d[l], s5_glu_w[l], s5_glu_b[l])
        u_rows = us.transpose(1, 0, 2).reshape(tc * b, W_S5)
        ys = _s5_calls(u_rows, s5p, b, n_tok, n_ctx).reshape(tc, b, W_S5).transpose(1, 0, 2)

        gate_b = jnp.concatenate([ml_igate_b[l, 0], ml_fgate_b[l, 0], ml_igate_b[l, 1], ml_fgate_b[l, 1]])
        bias_row = jnp.tile(gate_b, LANES // N_GATES).reshape(1, LANES)
        bias_col = jnp.broadcast_to(gate_b[:, None], (N_GATES, LANES))
        ym = _mlstm_calls(qkvm, kmt, gm, gmt, om, bias_row, bias_col, ml_norm_g[l], n_tok, n_ctx)

        wb = [w.astype(BF16) for w in (w_branch_attn[l], w_branch_s5[l], w_branch_ml[l], w_out[l])]
        wf = (ffn_w_up[l].astype(BF16), ffn_conv_w[l], ffn_conv_b[l], ffn_w_down[l].astype(BF16), final_norm_g)
        x1, h2 = _merge_call(ya, ys, ym, gb, xl, modsel, b, norm2_g[l], *wb, ROW_TILE, 0)
        xl = _ffn_call(h2, x1, modsel, b, *wf, FFN_TILE, final=final)
        if not final:
            x1, h2 = _merge_call(ya_ctx, ys, ym, gb, xk, modsel, 0, norm2_g[l], *wb, n_ctx, n_tok // n_ctx)
            xk = _ffn_call(h2, x1, modsel, 0, *wf, n_ctx, final=False)
    return xl
```
